```python
import math
import functools
import jax
import jax.numpy as jnp
from jax import lax
import numpy as np

D_MODEL = 1024
BATCH = 2
SEQ = 8192
DEPTH = 1
DEC_BATCH = 128
DEC_SEQ = 4
PAST_LEN = 8192
PAGE_SIZE = 128

N_HEADS = 8
HEAD_DIM = 128
KV_HEADS = 2
Q_PER_KV = N_HEADS // KV_HEADS
ATTN_W = N_HEADS * HEAD_DIM
KV_W = KV_HEADS * HEAD_DIM
ROPE_THETA = 500000.0
IDX_HEADS = 8
IDX_DIM = 64
TOPK_MAX = 256
Q_BLOCK = 128
D_INNER = 2 * D_MODEL
SSM_HEAD_DIM = 64
SSM_HEADS = D_INNER // SSM_HEAD_DIM
SSM_GROUPS = 4
HEADS_PER_GROUP = SSM_HEADS // SSM_GROUPS
D_STATE = 128
CONV_W = 4
CONV_DIM = D_INNER + 2 * SSM_GROUPS * D_STATE
SSD_CHUNK = 128
N_EXPERT_GROUPS = 4
EXPERTS_PER_GROUP = 8
N_EXPERTS = N_EXPERT_GROUPS * EXPERTS_PER_GROUP
TOP_K_EXPERTS = 2
EXPERT_FF = 256
PLE_DIM = 256
EPS = 1e-6
IN_SPLITS = (ATTN_W, KV_W, KV_W, IDX_HEADS * IDX_DIM, IDX_DIM, IDX_HEADS, D_INNER, CONV_DIM, SSM_HEADS, D_MODEL, D_MODEL)
IN_COLS = sum(IN_SPLITS)

kernel_name = 'hybrid_dsa_ssd_hmoe_step'


def _rms(x):
    xf = x.astype(jnp.float32)
    return (xf * lax.rsqrt(jnp.mean(xf * xf, axis=-1, keepdims=True) + EPS)).astype(x.dtype)


def rms_norm(x, g):
    return _rms(x) * g


def rope_partial(x, pos):
    rot = x.shape[-1] // 4
    half = rot // 2
    inv = ROPE_THETA ** (-(jnp.arange(half, dtype=jnp.float32) * 2.0 / rot))
    ang = pos.astype(jnp.float32)[:, None] * inv[None, :]
    cos = jnp.cos(ang)[None, :, None, :]
    sin = jnp.sin(ang)[None, :, None, :]
    xf = x.astype(jnp.float32)
    x1 = xf[..., :half]
    x2 = xf[..., half:rot]
    return jnp.concatenate([x1 * cos - x2 * sin, x2 * cos + x1 * sin, xf[..., rot:]], axis=-1).astype(x.dtype)


def index_scores(qi, wi, keys, q_pos, k_pos):
    s = jnp.einsum('bthd,bsd->bths', qi, keys, preferred_element_type=jnp.float32)
    s = jnp.einsum('bths,bth->bts', jax.nn.relu(s), wi.astype(jnp.float32))
    return jnp.where(k_pos[None, None, :] <= q_pos[None, :, None], s, -jnp.inf)


def attend_selected(q, k_sel, v_sel, valid):
    b, t = q.shape[:2]
    qg = q.reshape(b, t, KV_HEADS, Q_PER_KV, HEAD_DIM)
    s = jnp.einsum('btgrd,btkgd->btgrk', qg, k_sel, preferred_element_type=jnp.float32) * (HEAD_DIM ** -0.5)
    s = jnp.where(valid[:, :, None, None, :], s, -jnp.inf)
    p = jax.nn.softmax(s, axis=-1).astype(v_sel.dtype)
    o = jnp.einsum('btgrk,btkgd->btgrd', p, v_sel)
    return o.reshape(b, t, ATTN_W)


def sparse_attn_prompt(q, k, v, qi, ki, wi):
    b, t = q.shape[:2]
    topk = min(TOPK_MAX, t // 4)
    nblk = t // Q_BLOCK
    k_pos = jnp.arange(t)
    bidx = jnp.arange(b)[:, None, None]

    def block(args):
        qb, qib, wib, start = args
        q_pos = start + jnp.arange(Q_BLOCK)
        sc = index_scores(qib, wib, ki, q_pos, k_pos)
        _, sel = lax.top_k(sc, topk)
        valid = sel <= q_pos[None, :, None]
        return attend_selected(qb, k[bidx, sel], v[bidx, sel], valid)

    def to_blocks(a):
        return jnp.moveaxis(a.reshape(b, nblk, Q_BLOCK, *a.shape[2:]), 1, 0)

    out = lax.map(block, (to_blocks(q), to_blocks(qi), to_blocks(wi), jnp.arange(nblk) * Q_BLOCK))
    return jnp.moveaxis(out, 0, 1).reshape(b, t, ATTN_W)


def sparse_attn_sample(q, k, v, qi, ki, wi, cache_k, cache_v, cache_ik, page_table):
    b, t = q.shape[:2]
    n_pages = page_table.shape[1]
    past = n_pages * PAGE_SIZE
    total = past + t
    topk = min(TOPK_MAX, total // 4)
    past_ik = cache_ik[page_table].reshape(b, past, IDX_DIM)
    keys = jnp.concatenate([past_ik, ki.astype(past_ik.dtype)], axis=1)
    q_pos = past + jnp.arange(t)
    sc = index_scores(qi, wi, keys, q_pos, jnp.arange(total))
    _, sel = lax.top_k(sc, topk)
    valid = sel <= q_pos[None, :, None]
    in_past = sel < past
    bidx = jnp.arange(b)[:, None, None]
    phys = page_table[bidx, jnp.minimum(sel // PAGE_SIZE, n_pages - 1)]
    off = sel % PAGE_SIZE
    j_new = jnp.clip(sel - past, 0, t - 1)

    def pick(cache, new):
        return jnp.where(in_past[..., None, None], cache[phys, off], new[bidx, j_new])

    return attend_selected(q, pick(cache_k, k), pick(cache_v, v), valid)


def ssd_scan(x, dt, a, bm, cm, s0):
    b, t = x.shape[:2]
    q = min(SSD_CHUNK, t)
    pad = (-t) % q
    if pad:
        def padt(u):
            return jnp.pad(u, [(0, 0), (0, pad)] + [(0, 0)] * (u.ndim - 2))
        x, dt, bm, cm = padt(x), padt(dt), padt(bm), padt(cm)
    nc = (t + pad) // q
    x = x.reshape(b, nc, q, SSM_GROUPS, HEADS_PER_GROUP, SSM_HEAD_DIM)
    dt = dt.reshape(b, nc, q, SSM_GROUPS, HEADS_PER_GROUP)
    bm = bm.reshape(b, nc, q, SSM_GROUPS, D_STATE)
    cm = cm.reshape(b, nc, q, SSM_GROUPS, D_STATE)
    xd = x * dt[..., None]
    a_cum = jnp.cumsum(dt * a.reshape(SSM_GROUPS, HEADS_PER_GROUP), axis=2)
    causal = jnp.tril(jnp.ones((q, q), dtype=bool))
    seg = a_cum[:, :, :, None] - a_cum[:, :, None, :]
    decay_ls = jnp.exp(jnp.where(causal[:, :, None, None], seg, -jnp.inf))
    cb = jnp.einsum('bclgn,bcsgn->bclsg', cm, bm)
    y_diag = jnp.einsum('bclsgr,bcsgrp->bclgrp', cb[..., None] * decay_ls, xd)
    decay_end = jnp.exp(a_cum[:, :, -1:] - a_cum)
    chunk_states = jnp.einsum('bclgn,bclgrp->bcgrpn', bm, xd * decay_end[..., None])
    chunk_decay = jnp.exp(a_cum[:, :, -1])

    def step(s, inp):
        st, dc = inp
        return s * dc[..., None, None] + st, s

    s_final, s_prev = lax.scan(step, s0.reshape(b, SSM_GROUPS, HEADS_PER_GROUP, SSM_HEAD_DIM, D_STATE),
                               (jnp.moveaxis(chunk_states, 1, 0), jnp.moveaxis(chunk_decay, 1, 0)))
    s_prev = jnp.moveaxis(s_prev, 0, 1)
    y_off = jnp.einsum('bclgn,bcgrpn->bclgrp', cm, s_prev) * jnp.exp(a_cum)[..., None]
    y = (y_diag + y_off).reshape(b, nc * q, SSM_HEADS, SSM_HEAD_DIM)[:, :t]
    return y, s_final.reshape(b, SSM_HEADS, SSM_HEAD_DIM, D_STATE)


def mamba_branch(z, xbc, dt_raw, conv_state, ssm_state, prm):
    b, t = xbc.shape[:2]
    xpad = jnp.concatenate([conv_state.astype(xbc.dtype), xbc], axis=1)
    conv = prm['conv_b'] + sum(xpad[:, j:j + t] * prm['conv_w'][j] for j in range(CONV_W))
    act = jax.nn.silu(conv)
    xs = act[..., :D_INNER].reshape(b, t, SSM_HEADS, SSM_HEAD_DIM)
    bm = act[..., D_INNER:D_INNER + SSM_GROUPS * D_STATE].reshape(b, t, SSM_GROUPS, D_STATE)
    cm = act[..., D_INNER + SSM_GROUPS * D_STATE:].reshape(b, t, SSM_GROUPS, D_STATE)
    dt = jax.nn.softplus(dt_raw.astype(jnp.float32) + prm['dt_bias'].astype(jnp.float32))
    a = -jnp.exp(prm['a_log'].astype(jnp.float32))
    y, s_new = ssd_scan(xs.astype(jnp.float32), dt, a, bm.astype(jnp.float32), cm.astype(jnp.float32),
                        ssm_state.astype(jnp.float32))
    y = y + prm['d_skip'].astype(jnp.float32)[:, None] * xs.astype(jnp.float32)
    y = y.reshape(b, t, D_INNER).astype(z.dtype) * jax.nn.silu(z)
    y = _rms(y.reshape(b, t, SSM_GROUPS, D_INNER // SSM_GROUPS)).reshape(b, t, D_INNER) * prm['g_ssm_norm']
    return y @ prm['w_ssm_out'], xpad[:, t:], s_new.astype(z.dtype)


def hier_moe(h, prm):
    shp = h.shape
    hf = h.reshape(-1, D_MODEL)
    n = hf.shape[0]
    g_logits = (hf @ prm['w_group_router'] + prm['b_group_router']).astype(jnp.float32)
    g_prob = jax.nn.softmax(g_logits, axis=-1)
    g_sel = jnp.argmax(g_logits, axis=-1)
    e_logits = (hf @ prm['w_expert_router'] + prm['b_expert_router']).astype(jnp.float32)
    e_logits = e_logits.reshape(n, N_EXPERT_GROUPS, EXPERTS_PER_GROUP)
    e_in_group = jnp.take_along_axis(e_logits, g_sel[:, None, None], axis=1)[:, 0]
    top_p, top_i = lax.top_k(jax.nn.softmax(e_in_group, axis=-1), TOP_K_EXPERTS)
    wts = top_p / jnp.sum(top_p, axis=-1, keepdims=True) * jnp.take_along_axis(g_prob, g_sel[:, None], axis=1)
    expert_id = g_sel[:, None] * EXPERTS_PER_GROUP + top_i
    gate = jnp.sum(jax.nn.one_hot(expert_id, N_EXPERTS, dtype=jnp.float32) * wts[..., None], axis=1).astype(h.dtype)
    out = jnp.zeros_like(hf)
    for e in range(N_EXPERTS):
        act = jax.nn.silu(hf @ prm['w_expert_gate'][e]) * (hf @ prm['w_expert_up'][e])
        out = out + gate[:, e:e + 1] * (act @ prm['w_expert_down'][e])
    return out.reshape(shp)


def layer_forward(x, p_emb, pos, attn_fn, conv_state, ssm_state, prm):
    b, t = x.shape[:2]
    h = rms_norm(x, prm['g_mix'])
    offs = [int(o) for o in np.cumsum(IN_SPLITS)[:-1]]
    q, k, v, qi, ki, wi, z, xbc, dt_raw, ga, gm = jnp.split(h @ prm['w_in'], offs, axis=-1)
    q = rope_partial(q.reshape(b, t, N_HEADS, HEAD_DIM), pos)
    k = rope_partial(k.reshape(b, t, KV_HEADS, HEAD_DIM), pos)
    v = v.reshape(b, t, KV_HEADS, HEAD_DIM)
    qi = rope_partial(qi.reshape(b, t, IDX_HEADS, IDX_DIM), pos)
    ki = rope_partial(ki[:, :, None, :], pos)[:, :, 0]
    wi = wi * ((IDX_HEADS * IDX_DIM) ** -0.5)
    a_out = attn_fn(q, k, v, qi, ki, wi) @ prm['w_attn_out']
    m_out, conv_new, ssm_new = mamba_branch(z, xbc, dt_raw, conv_state, ssm_state, prm)
    x = x + (jax.nn.sigmoid(ga) * a_out + jax.nn.sigmoid(gm) * m_out) @ prm['w_o']
    x = x + hier_moe(rms_norm(x, prm['g_ffn']), prm)
    x = x + (p_emb @ prm['w_ple']) * jax.nn.sigmoid(rms_norm(x, prm['g_ple']) @ prm['w_ple_gate'])
    return x, k, v, ki, conv_new, ssm_new


def setup_inputs(seed: int = 0) -> dict:
    key = jax.random.key(seed)
    ks = jax.random.split(key, 40)
    f32 = jnp.float32
    n_pages = PAST_LEN // PAGE_SIZE
    n_pool = (DEC_BATCH * n_pages * 5) // 4

    def nrm(k, shape, scale):
        return jax.random.normal(k, shape, f32) * scale

    dt0 = jnp.exp(jax.random.uniform(ks[20], (DEPTH, SSM_HEADS), f32, math.log(1e-3), math.log(1e-1)))
    perm = jax.random.permutation(ks[9], n_pool)
    return {
        'x_prompt': nrm(ks[0], (BATCH, SEQ, D_MODEL), 1.0),
        'x_sample': nrm(ks[1], (DEC_BATCH, DEC_SEQ, D_MODEL), 1.0),
        'cache_k': nrm(ks[2], (DEPTH, n_pool, PAGE_SIZE, KV_HEADS, HEAD_DIM), 1.0),
        'cache_v': nrm(ks[3], (DEPTH, n_pool, PAGE_SIZE, KV_HEADS, HEAD_DIM), 1.0),
        'cache_idx_k': nrm(ks[4], (DEPTH, n_pool, PAGE_SIZE, IDX_DIM), 1.0),
        'state_ssm': nrm(ks[5], (DEPTH, DEC_BATCH, SSM_HEADS, SSM_HEAD_DIM, D_STATE), 0.5),
        'state_conv': nrm(ks[6], (DEPTH, DEC_BATCH, CONV_W - 1, CONV_DIM), 1.0),
        'page_table': perm[:DEC_BATCH * n_pages].reshape(DEC_BATCH, n_pages).astype(jnp.int32),
        'p_prompt': nrm(ks[7], (DEPTH, BATCH, SEQ, PLE_DIM), 1.0),
        'p_sample': nrm(ks[8], (DEPTH, DEC_BATCH, DEC_SEQ, PLE_DIM), 1.0),
        'g_mix': 1.0 + nrm(ks[10], (DEPTH, D_MODEL), 0.02),
        'w_in': nrm(ks[11], (DEPTH, D_MODEL, IN_COLS), D_MODEL ** -0.5),
        'w_attn_out': nrm(ks[12], (DEPTH, ATTN_W, D_MODEL), ATTN_W ** -0.5),
        'conv_w': nrm(ks[13], (DEPTH, CONV_W, CONV_DIM), 0.5),
        'conv_b': nrm(ks[14], (DEPTH, CONV_DIM), 0.02),
        'dt_bias': dt0 + jnp.log(-jnp.expm1(-dt0)),
        'a_log': jnp.log(jax.random.uniform(ks[21], (DEPTH, SSM_HEADS), f32, 1.0, 16.0)),
        'd_skip': 1.0 + nrm(ks[15], (DEPTH, SSM_HEADS), 0.1),
        'g_ssm_norm': 1.0 + nrm(ks[16], (DEPTH, D_INNER), 0.02),
        'w_ssm_out': nrm(ks[17], (DEPTH, D_INNER, D_MODEL), D_INNER ** -0.5),
        'w_o': nrm(ks[18], (DEPTH, D_MODEL, D_MODEL), D_MODEL ** -0.5),
        'g_ffn': 1.0 + nrm(ks[19], (DEPTH, D_MODEL), 0.02),
        'w_group_router': nrm(ks[22], (DEPTH, D_MODEL, N_EXPERT_GROUPS), D_MODEL ** -0.5),
        'b_group_router': nrm(ks[23], (DEPTH, N_EXPERT_GROUPS), 0.01),
        'w_expert_router': nrm(ks[24], (DEPTH, D_MODEL, N_EXPERTS), D_MODEL ** -0.5),
        'b_expert_router': nrm(ks[25], (DEPTH, N_EXPERTS), 0.01),
        'w_expert_gate': nrm(ks[26], (DEPTH, N_EXPERTS, D_MODEL, EXPERT_FF), D_MODEL ** -0.5),
        'w_expert_up': nrm(ks[27], (DEPTH, N_EXPERTS, D_MODEL, EXPERT_FF), D_MODEL ** -0.5),
        'w_expert_down': nrm(ks[28], (DEPTH, N_EXPERTS, EXPERT_FF, D_MODEL), EXPERT_FF ** -0.5),
        'w_ple': nrm(ks[29], (DEPTH, PLE_DIM, D_MODEL), PLE_DIM ** -0.5),
        'g_ple': 1.0 + nrm(ks[30], (DEPTH, D_MODEL), 0.02),
        'w_ple_gate': nrm(ks[31], (DEPTH, D_MODEL, D_MODEL), D_MODEL ** -0.5),
        'g_final': 1.0 + nrm(ks[32], (D_MODEL,), 0.02),
    }


def reference(x_prompt, x_sample, cache_k, cache_v, cache_idx_k, state_ssm, state_conv, page_table,
              p_prompt, p_sample, g_mix, w_in, w_attn_out, conv_w, conv_b, dt_bias, a_log, d_skip,
              g_ssm_norm, w_ssm_out, w_o, g_ffn, w_group_router, b_group_router, w_expert_router,
              b_expert_router, w_expert_gate, w_expert_up, w_expert_down, w_ple, g_ple, w_ple_gate, g_final):
    bp, tp = x_prompt.shape[:2]
    ts = x_sample.shape[1]
    past = page_table.shape[1] * PAGE_SIZE
    pos_p = jnp.arange(tp)
    pos_s = past + jnp.arange(ts)
    conv0 = jnp.zeros((bp, CONV_W - 1, CONV_DIM), x_prompt.dtype)
    ssm0 = jnp.zeros((bp, SSM_HEADS, SSM_HEAD_DIM, D_STATE), x_prompt.dtype)
    xp, xs = x_prompt, x_sample
    kp_l, vp_l, ikp_l, ks_l, vs_l, iks_l = [], [], [], [], [], []
    ssmp_l, convp_l, ssms_l, convs_l = [], [], [], []
    for i in range(DEPTH):
        prm = {
            'g_mix': g_mix[i], 'w_in': w_in[i], 'w_attn_out': w_attn_out[i],
            'conv_w': conv_w[i], 'conv_b': conv_b[i], 'dt_bias': dt_bias[i], 'a_log': a_log[i],
            'd_skip': d_skip[i], 'g_ssm_norm': g_ssm_norm[i], 'w_ssm_out': w_ssm_out[i], 'w_o': w_o[i],
            'g_ffn': g_ffn[i], 'w_group_router': w_group_router[i], 'b_group_router': b_group_router[i],
            'w_expert_router': w_expert_router[i], 'b_expert_router': b_expert_router[i],
            'w_expert_gate': w_expert_gate[i], 'w_expert_up': w_expert_up[i], 'w_expert_down': w_expert_down[i],
            'w_ple': w_ple[i], 'g_ple': g_ple[i], 'w_ple_gate': w_ple_gate[i],
        }
        xp, kp, vp, ikp, convp, ssmp = layer_forward(xp, p_prompt[i], pos_p, sparse_attn_prompt, conv0, ssm0, prm)
        attn_s = functools.partial(sparse_attn_sample, cache_k=cache_k[i], cache_v=cache_v[i],
                                   cache_ik=cache_idx_k[i], page_table=page_table)
        xs, ks_, vs_, iks, convs, ssms = layer_forward(xs, p_sample[i], pos_s, attn_s, state_conv[i], state_ssm[i], prm)
        kp_l.append(kp); vp_l.append(vp); ikp_l.append(ikp)
        ks_l.append(ks_); vs_l.append(vs_); iks_l.append(iks)
        ssmp_l.append(ssmp); convp_l.append(convp); ssms_l.append(ssms); convs_l.append(convs)
    y_prompt = rms_norm(xp, g_final)
    y_sample = rms_norm(xs, g_final)
    return (y_prompt, y_sample,
            jnp.stack(kp_l), jnp.stack(vp_l), jnp.stack(ikp_l),
            jnp.stack(ks_l), jnp.stack(vs_l), jnp.stack(iks_l),
            jnp.stack(ssmp_l), jnp.stack(convp_l), jnp.stack(ssms_l), jnp.stack(convs_l))
```

```python
import functools

import jax
import jax.numpy as jnp
import numpy as np
from jax import lax
from jax.experimental import pallas as pl
from jax.experimental.pallas import tpu as pltpu

F32 = jnp.float32
BF16 = jnp.bfloat16
I32 = jnp.int32

D_MODEL = 1024
N_HEADS = 8
HEAD_DIM = 128
KV_HEADS = 2
Q_PER_KV = N_HEADS // KV_HEADS
ATTN_W = N_HEADS * HEAD_DIM
KV_W = KV_HEADS * HEAD_DIM
ROPE_THETA = 500000.0
IDX_HEADS = 8
IDX_DIM = 64
IDX_W = IDX_HEADS * IDX_DIM
TOPK_MAX = 256
Q_BLOCK = 128
PAGE_SIZE = 128
D_INNER = 2 * D_MODEL
SSM_HEAD_DIM = 64
SSM_HEADS = D_INNER // SSM_HEAD_DIM
SSM_GROUPS = 4
HEADS_PER_GROUP = SSM_HEADS // SSM_GROUPS
D_STATE = 128
CONV_W = 4
CONV_DIM = D_INNER + 2 * SSM_GROUPS * D_STATE
SSD_CHUNK = 128
N_EXPERT_GROUPS = 4
EXPERTS_PER_GROUP = 8
N_EXPERTS = N_EXPERT_GROUPS * EXPERTS_PER_GROUP
EXPERT_FF = 256
PLE_DIM = 256
EPS = 1e-6
IN_SPLITS = (ATTN_W, KV_W, KV_W, IDX_W, IDX_DIM, IDX_HEADS, D_INNER, CONV_DIM, SSM_HEADS, D_MODEL, D_MODEL)

LANES = 128
SUBLANES = 8
VMEM_LIMIT = 56 * 1024 * 1024
INT_MIN = -2147483648
NEG_BIG = -1e30
CUT_ALL = 1 << 20
PAGES_PER_STEP = 16

A_Q, A_K, A_V, A_QI, A_KIW = 0, ATTN_W, ATTN_W + KV_W, ATTN_W + 2 * KV_W, ATTN_W + 2 * KV_W + IDX_W
A_COLS = A_KIW + LANES
B_Z, B_XBC, B_DT = 0, D_INNER, D_INNER + CONV_DIM
B_COLS = B_DT + LANES


def _cparams(sem):
    return pltpu.CompilerParams(dimension_semantics=sem, vmem_limit_bytes=VMEM_LIMIT)


def _rms(x):
    return x * lax.rsqrt(jnp.mean(x * x, axis=-1, keepdims=True) + EPS)


def _dot(a, b):
    return jnp.dot(a, b, preferred_element_type=F32)


def _dot_nt(a, b):
    return lax.dot_general(a, b, (((1,), (1,)), ((), ())), preferred_element_type=F32)


def _dot_tn(a, b):
    return lax.dot_general(a, b, (((0,), (0,)), ((), ())), preferred_element_type=F32)


def _silu(x):
    return x * jax.nn.sigmoid(x)


def _rope(y, tab_ref, base, shift):
    return (y * tab_ref[base] + pltpu.roll(y, LANES - shift, 1) * tab_ref[base + 1]
            + pltpu.roll(y, shift, 1) * tab_ref[base + 2])


def _proj_attn_kernel(x_ref, g_ref, w_ref, tab_ref, q_ref, kf_ref, vf_ref, kb_ref, vb_ref, qi_ref, kiw_ref, kib_ref):
    h = (_rms(x_ref[0]) * g_ref[...]).astype(BF16)
    yq = _dot(h, w_ref[:, A_Q:A_K])
    for hh in range(N_HEADS):
        q_ref[0, hh] = _rope(yq[:, hh * LANES:(hh + 1) * LANES], tab_ref, 0, 16).astype(BF16)
    ykv = _dot(h, w_ref[:, A_K:A_QI])
    for hh in range(KV_HEADS):
        kr = _rope(ykv[:, hh * LANES:(hh + 1) * LANES], tab_ref, 0, 16)
        kf_ref[0, :, hh * LANES:(hh + 1) * LANES] = kr
        kb_ref[0, :, hh * LANES:(hh + 1) * LANES] = kr.astype(BF16)
    v = ykv[:, KV_W:]
    vf_ref[0] = v
    vb_ref[0] = v.astype(BF16)
    yi = _dot(h, w_ref[:, A_QI:A_COLS])
    for c in range(IDX_W // LANES):
        r = _rope(yi[:, c * LANES:(c + 1) * LANES], tab_ref, 3, 8).astype(BF16)
        qi_ref[0, 2 * c] = r[:, :IDX_DIM]
        qi_ref[0, 2 * c + 1] = r[:, IDX_DIM:]
    kiw = _rope(yi[:, IDX_W:], tab_ref, 6, 8)
    kiw_ref[0] = kiw
    kib_ref[0] = kiw[:, :IDX_DIM].astype(BF16)


def _proj_ssm_kernel(x_ref, g_ref, w_ref, dtb_ref, z_ref, xbc_ref, dt_ref):
    h = (_rms(x_ref[0]) * g_ref[...]).astype(BF16)
    z_ref[0] = _dot(h, w_ref[:, B_Z:B_XBC])
    xbc_ref[0] = _dot(h, w_ref[:, B_XBC:B_DT])
    dt_ref[0] = jax.nn.softplus(_dot(h, w_ref[:, B_DT:B_COLS]) + dtb_ref[...])


def _proj_gate_kernel(x_ref, g_ref, w_ref, ga_ref, gm_ref):
    h = (_rms(x_ref[0]) * g_ref[...]).astype(BF16)
    y = _dot(h, w_ref[...])
    ga_ref[0] = jax.nn.sigmoid(y[:, :D_MODEL])
    gm_ref[0] = jax.nn.sigmoid(y[:, D_MODEL:])


def _const_spec(shape):
    nd = len(shape)
    return pl.BlockSpec(shape, lambda *_: (0,) * nd, pipeline_mode=pl.Buffered(1))


def _in_proj(x, g_mix, w_a, w_b, w_c, tabs, dtb, tm):
    bsz, t, _ = x.shape
    grid = (bsz, t // tm)
    xs = pl.BlockSpec((1, tm, D_MODEL), lambda b, i: (b, i, 0))
    gs = _const_spec((1, D_MODEL))

    def tok(width):
        return pl.BlockSpec((1, tm, width), lambda b, i: (b, i, 0))

    def hm(nh, width):
        return pl.BlockSpec((1, nh, tm, width), lambda b, i: (b, 0, i, 0))

    sd = jax.ShapeDtypeStruct
    attn = pl.pallas_call(
        _proj_attn_kernel, grid=grid, name="proj_attn",
        in_specs=[xs, gs, _const_spec((D_MODEL, A_COLS)), pl.BlockSpec((9, tm, LANES), lambda b, i: (0, i, 0))],
        out_specs=[hm(N_HEADS, HEAD_DIM), tok(KV_W), tok(KV_W), tok(KV_W), tok(KV_W), hm(IDX_HEADS, IDX_DIM),
                   tok(LANES), tok(IDX_DIM)],
        out_shape=[sd((bsz, N_HEADS, t, HEAD_DIM), BF16), sd((bsz, t, KV_W), F32), sd((bsz, t, KV_W), F32),
                   sd((bsz, t, KV_W), BF16), sd((bsz, t, KV_W), BF16), sd((bsz, IDX_HEADS, t, IDX_DIM), BF16),
                   sd((bsz, t, LANES), F32), sd((bsz, t, IDX_DIM), BF16)],
        compiler_params=_cparams(("parallel", "parallel")),
    )(x, g_mix, w_a, tabs)
    ssm = pl.pallas_call(
        _proj_ssm_kernel, grid=grid, name="proj_ssm",
        in_specs=[xs, gs, _const_spec((D_MODEL, B_COLS)), _const_spec((1, LANES))],
        out_specs=[tok(D_INNER), tok(CONV_DIM), tok(LANES)],
        out_shape=[sd((bsz, t, D_INNER), F32), sd((bsz, t, CONV_DIM), F32), sd((bsz, t, LANES), F32)],
        compiler_params=_cparams(("parallel", "parallel")),
    )(x, g_mix, w_b, dtb)
    gates = pl.pallas_call(
        _proj_gate_kernel, grid=grid, name="proj_gate",
        in_specs=[xs, gs, _const_spec((D_MODEL, 2 * D_MODEL))],
        out_specs=[tok(D_MODEL), tok(D_MODEL)],
        out_shape=[sd((bsz, t, D_MODEL), F32), sd((bsz, t, D_MODEL), F32)],
        compiler_params=_cparams(("parallel", "parallel")),
    )(x, g_mix, w_c)
    return attn, ssm, gates


def _sortable_key(s):
    s = jnp.where(s == 0.0, 0.0, s)
    b = lax.bitcast_convert_type(s, I32)
    return b ^ ((b >> 31) & 0x7FFFFFFF)


def _count(keys_ref, nchunk, cw, pred):
    rows = keys_ref.shape[0]

    def body(c, acc):
        off = pl.multiple_of(c * cw, cw)
        for j in range(cw // LANES):
            k = keys_ref[:, pl.ds(off + j * LANES, LANES)]
            acc = acc + jnp.where(pred(k, off + j * LANES), 1.0, 0.0)
        return acc

    acc = lax.fori_loop(0, nchunk, body, jnp.zeros((rows, LANES), F32))
    return jnp.sum(acc, axis=1, keepdims=True)


def _select_threshold(keys_ref, nchunk, cw, topk):
    rows = keys_ref.shape[0]
    kf = float(topk)

    def bcast(v):
        return jnp.broadcast_to(v, (rows, LANES))

    def cnt_ge(cand):
        cb = bcast(cand)
        return _count(keys_ref, nchunk, cw, lambda k, off: k >= cb)

    c0 = cnt_ge(jnp.zeros((rows, 1), I32))
    t0 = jnp.where(c0 >= kf, 0, INT_MIN).astype(I32)

    def bit_body(j, t):
        cand = t | (jnp.int32(1) << (30 - j))
        return jnp.where(cnt_ge(cand) >= kf, cand, t)

    thr = lax.fori_loop(0, 31, bit_body, t0)
    thr_b = bcast(thr)
    n_gt = _count(keys_ref, nchunk, cw, lambda k, off: k > thr_b)
    n_eq = _count(keys_ref, nchunk, cw, lambda k, off: k == thr_b)
    need = kf - n_gt
    tie = jnp.logical_and(n_eq > need, thr != INT_MIN)
    lane = lax.broadcasted_iota(I32, (rows, LANES), 1)

    def tie_cut():
        def body(j, u):
            cand = u | (jnp.int32(1) << (13 - j))
            cb = bcast(cand)
            c = _count(keys_ref, nchunk, cw, lambda k, off: jnp.logical_and(k == thr_b, lane + off < cb))
            return jnp.where(c < need, cand, u)

        u = lax.fori_loop(0, 14, body, jnp.zeros((rows, 1), I32))
        return jnp.where(tie, u, CUT_ALL).astype(I32)

    any_tie = jnp.max(jnp.where(tie, 1.0, 0.0)) > 0.0
    cut = lax.cond(any_tie, tie_cut, lambda: jnp.full((rows, 1), CUT_ALL, I32))
    return thr, cut


def _selected(k, col, thr_b, cut_b):
    tie = jnp.logical_and(k == thr_b, col <= cut_b)
    return jnp.logical_and(jnp.logical_or(k > thr_b, tie), k != INT_MIN)


def _index_scores(qi, wexp, keys, qb, stacked):
    ncol = keys.shape[0] // LANES
    cols = [jnp.zeros((qb, LANES), F32)] * ncol
    if stacked:
        s_all = _dot_nt(qi, keys)
    for hh in range(IDX_HEADS):
        rs = slice(hh * qb, (hh + 1) * qb)
        s = s_all[rs] if stacked else _dot_nt(qi[rs], keys)
        w = wexp[rs]
        cols = [a + jnp.maximum(s[:, j * LANES:(j + 1) * LANES], 0.0) * w for j, a in enumerate(cols)]
    return cols[0] if ncol == 1 else jnp.concatenate(cols, axis=1)


def _softmax_step(s, sel, m_ref, l_ref, acc_ref, rows, v):
    m_old = m_ref[rows]
    m_new = jnp.maximum(m_old, jnp.max(jnp.where(sel, s, NEG_BIG), axis=1, keepdims=True))
    p = jnp.where(sel, jnp.exp(s - m_new), 0.0)
    alpha = jnp.exp(m_old - m_new)
    l_ref[rows] = alpha * l_ref[rows] + jnp.sum(p, axis=1, keepdims=True)
    acc_ref[rows] = alpha * acc_ref[rows] + _dot(p.astype(BF16), v)
    m_ref[rows] = m_new


def _attn_prompt_kernel(q_ref, qi_ref, kiw_ref, kib_ref, kb_ref, vb_ref, o_ref,
                        keys_ref, wexp_ref, m_ref, l_ref, acc_ref, *, kc, topk):
    i = pl.program_id(1)
    qb = Q_BLOCK
    nch = (i * qb) // kc + 1
    rowpos = i * qb + lax.broadcasted_iota(I32, (qb, kc), 0)
    lanecol = lax.broadcasted_iota(I32, (qb, kc), 1)

    kiw = kiw_ref[0]
    for hh in range(IDX_HEADS):
        wexp_ref[hh * qb:(hh + 1) * qb, :] = jnp.broadcast_to(kiw[:, IDX_DIM + hh:IDX_DIM + hh + 1], (qb, LANES))
    qi = qi_ref[0].reshape(IDX_HEADS * qb, IDX_DIM)

    def score_body(c, carry):
        off = pl.multiple_of(c * kc, kc)
        sc = _index_scores(qi, wexp_ref[...], kib_ref[0, pl.ds(off, kc), :], qb, stacked=False)
        keys_ref[:, pl.ds(off, kc)] = jnp.where(lanecol + off <= rowpos, _sortable_key(sc), INT_MIN)
        return carry

    lax.fori_loop(0, nch, score_body, 0)
    thr, cut = _select_threshold(keys_ref, nch, kc, topk)
    thr_b = jnp.broadcast_to(thr, (qb, kc))
    cut_b = jnp.broadcast_to(cut, (qb, kc))

    m_ref[...] = jnp.full(m_ref.shape, NEG_BIG, F32)
    l_ref[...] = jnp.zeros(l_ref.shape, F32)
    acc_ref[...] = jnp.zeros(acc_ref.shape, F32)
    scale = HEAD_DIM ** -0.5

    def attn_body(c, carry):
        off = pl.multiple_of(c * kc, kc)
        sel = _selected(keys_ref[:, pl.ds(off, kc)], lanecol + off, thr_b, cut_b)
        for g in range(KV_HEADS):
            qg = q_ref[0, g * Q_PER_KV:(g + 1) * Q_PER_KV].reshape(Q_PER_KV * qb, HEAD_DIM)
            s = _dot_nt(qg, kb_ref[0, pl.ds(off, kc), g * HEAD_DIM:(g + 1) * HEAD_DIM]) * scale
            v = vb_ref[0, pl.ds(off, kc), g * HEAD_DIM:(g + 1) * HEAD_DIM]
            for r in range(Q_PER_KV):
                hh = g * Q_PER_KV + r
                _softmax_step(s[r * qb:(r + 1) * qb], sel, m_ref, l_ref, acc_ref, slice(hh * qb, (hh + 1) * qb), v)
        return carry

    lax.fori_loop(0, nch, attn_body, 0)
    for hh in range(N_HEADS):
        rows = slice(hh * qb, (hh + 1) * qb)
        o_ref[0, :, hh * HEAD_DIM:(hh + 1) * HEAD_DIM] = (acc_ref[rows] / l_ref[rows]).astype(o_ref.dtype)


def _attn_prompt(q_hm, qi_hm, kiw, ki_bf, k_bf, v_bf, kc=512):
    bsz, _, t, _ = q_hm.shape
    topk = min(TOPK_MAX, t // 4)
    kc = min(kc, t)
    qb = Q_BLOCK
    kern = functools.partial(_attn_prompt_kernel, kc=kc, topk=topk)
    return pl.pallas_call(
        kern, grid=(bsz, t // qb), name="attn_prompt",
        in_specs=[pl.BlockSpec((1, N_HEADS, qb, HEAD_DIM), lambda b, i: (b, 0, i, 0)),
                  pl.BlockSpec((1, IDX_HEADS, qb, IDX_DIM), lambda b, i: (b, 0, i, 0)),
                  pl.BlockSpec((1, qb, LANES), lambda b, i: (b, i, 0)),
                  pl.BlockSpec((1, t, IDX_DIM), lambda b, i: (b, 0, 0)),
                  pl.BlockSpec((1, t, KV_W), lambda b, i: (b, 0, 0)),
                  pl.BlockSpec((1, t, KV_W), lambda b, i: (b, 0, 0))],
        out_specs=pl.BlockSpec((1, qb, ATTN_W), lambda b, i: (b, i, 0)),
        out_shape=jax.ShapeDtypeStruct((bsz, t, ATTN_W), BF16),
        scratch_shapes=[pltpu.VMEM((qb, t), I32), pltpu.VMEM((IDX_HEADS * qb, LANES), F32),
                        pltpu.VMEM((N_HEADS * qb, 1), F32), pltpu.VMEM((N_HEADS * qb, 1), F32),
                        pltpu.VMEM((N_HEADS * qb, HEAD_DIM), F32)],
        compiler_params=_cparams(("parallel", "arbitrary")),
    )(q_hm, qi_hm, kiw, ki_bf, k_bf, v_bf)


TQ = SUBLANES


def _sample_scores_kernel(pt_ref, qi_ref, wexp_ref, kin_ref, *rest):
    pages = rest[:PAGES_PER_STEP]
    sp_ref, sn_ref, kbuf = rest[PAGES_PER_STEP:]
    for j, pg in enumerate(pages):
        kbuf[j * PAGE_SIZE:(j + 1) * PAGE_SIZE, :] = pg[0].astype(BF16)
    sp_ref[0] = _index_scores(qi_ref[0], wexp_ref[0], kbuf[...], TQ, stacked=True)

    @pl.when(pl.program_id(1) == pl.num_programs(1) - 1)
    def _():
        sn_ref[0] = _index_scores(qi_ref[0], wexp_ref[0], kin_ref[0], TQ, stacked=True)


def _sample_thr_kernel(sp_ref, sn_ref, thr_ref, cut_ref, keys_ref, *, n_past, ts, topk, cw):
    nb = sp_ref.shape[0]
    rows = nb * TQ
    live = lax.broadcasted_iota(I32, (rows, n_past), 0) % TQ < ts
    keys_ref[:, :n_past] = jnp.where(live, _sortable_key(sp_ref[...].reshape(rows, n_past)), INT_MIN)
    trow = lax.broadcasted_iota(I32, (rows, LANES), 0) % TQ
    lane = lax.broadcasted_iota(I32, (rows, LANES), 1)
    ok = jnp.logical_and(lane <= trow, trow < ts)
    keys_ref[:, n_past:] = jnp.where(ok, _sortable_key(sn_ref[...].reshape(rows, LANES)), INT_MIN)
    thr, cut = _select_threshold(keys_ref, (n_past + LANES) // cw, cw, topk)
    thr_ref[...] = jnp.broadcast_to(thr, (rows, LANES)).reshape(nb, TQ, LANES)
    cut_ref[...] = jnp.broadcast_to(cut, (rows, LANES)).reshape(nb, TQ, LANES)


def _sample_attn_kernel(pt_ref, q_ref, sp_ref, sn_ref, thr_ref, cut_ref, kn_ref, vn_ref, *rest, ts, n_past):
    kpages = rest[:PAGES_PER_STEP]
    vpages = rest[PAGES_PER_STEP:2 * PAGES_PER_STEP]
    o_ref, kbuf, vbuf, m_ref, l_ref, acc_ref = rest[2 * PAGES_PER_STEP:]
    pg = pl.program_id(1)
    last = pl.num_programs(1) - 1
    kc = PAGES_PER_STEP * PAGE_SIZE
    rows_g = Q_PER_KV * TQ
    scale = HEAD_DIM ** -0.5

    @pl.when(pg == 0)
    def _():
        m_ref[...] = jnp.full(m_ref.shape, NEG_BIG, F32)
        l_ref[...] = jnp.zeros(l_ref.shape, F32)
        acc_ref[...] = jnp.zeros(acc_ref.shape, F32)

    for j in range(PAGES_PER_STEP):
        kbuf[j * PAGE_SIZE:(j + 1) * PAGE_SIZE, :] = kpages[j][0].astype(BF16)
        vbuf[j * PAGE_SIZE:(j + 1) * PAGE_SIZE, :] = vpages[j][0].astype(BF16)

    def attend(sel8, kk, vv):
        sel = jnp.concatenate([sel8] * Q_PER_KV, axis=0)
        for g in range(KV_HEADS):
            cs = slice(g * HEAD_DIM, (g + 1) * HEAD_DIM)
            s = _dot_nt(q_ref[0, g * rows_g:(g + 1) * rows_g], kk[:, cs]) * scale
            _softmax_step(s, sel, m_ref, l_ref, acc_ref, slice(g * rows_g, (g + 1) * rows_g), vv[:, cs])

    thr_b = jnp.concatenate([thr_ref[0]] * (kc // LANES), axis=1)
    cut_b = jnp.concatenate([cut_ref[0]] * (kc // LANES), axis=1)
    col = lax.broadcasted_iota(I32, (TQ, kc), 1) + pg * kc
    sel_past = _selected(_sortable_key(sp_ref[0]), col, thr_b, cut_b)
    attend(sel_past, kbuf[...], vbuf[...])

    @pl.when(pg == last)
    def _():
        trow = lax.broadcasted_iota(I32, (TQ, LANES), 0)
        lane = lax.broadcasted_iota(I32, (TQ, LANES), 1)
        ok = jnp.logical_and(lane <= trow, trow < ts)
        kn = jnp.where(ok, _sortable_key(sn_ref[0]), INT_MIN)
        sel_new = _selected(kn, lane + n_past, thr_ref[0], cut_ref[0])
        attend(sel_new, kn_ref[0].astype(BF16), vn_ref[0].astype(BF16))
        o_ref[0] = (acc_ref[...] / l_ref[...]).astype(o_ref.dtype)


def _attn_sample(q_hm, qi_hm, kiw, k_f, v_f, cache_k, cache_v, cache_ik, page_table):
    nb, n_pages = page_table.shape
    ts = q_hm.shape[2] // nb
    n_pool = cache_k.shape[0]
    n_past = n_pages * PAGE_SIZE
    topk = min(TOPK_MAX, (n_past + ts) // 4)
    steps = n_pages // PAGES_PER_STEP
    kc = PAGES_PER_STEP * PAGE_SIZE

    def rows_ht(a, nh):
        d = a.shape[-1]
        a = a[0].reshape(nh, nb, ts, d).transpose(1, 0, 2, 3)
        return jnp.pad(a, ((0, 0), (0, 0), (0, TQ - ts), (0, 0))).reshape(nb, nh * TQ, d)

    q_r = rows_ht(q_hm, N_HEADS)
    qi_r = rows_ht(qi_hm, IDX_HEADS)
    wi = kiw[0, :, IDX_DIM:IDX_DIM + IDX_HEADS].reshape(nb, ts, IDX_HEADS).transpose(0, 2, 1)
    wexp = jnp.broadcast_to(jnp.pad(wi, ((0, 0), (0, 0), (0, TQ - ts)))[..., None], (nb, IDX_HEADS, TQ, LANES))
    wexp = wexp.reshape(nb, IDX_HEADS * TQ, LANES)

    def new_page(a):
        return jnp.pad(a[0].reshape(nb, ts, a.shape[-1]), ((0, 0), (0, PAGE_SIZE - ts), (0, 0)))

    ki_new = new_page(kiw[..., :IDX_DIM]).astype(BF16)
    k_new = new_page(k_f)
    v_new = new_page(v_f)
    ck = cache_k.reshape(n_pool, PAGE_SIZE, KV_W)
    cv = cache_v.reshape(n_pool, PAGE_SIZE, KV_W)

    def page_spec(width, j):
        return pl.BlockSpec((1, PAGE_SIZE, width), lambda b, p, pt: (pt[b, p * PAGES_PER_STEP + j], 0, 0))

    def per_b(shape):
        nd = len(shape)
        return pl.BlockSpec((1,) + shape, lambda b, p, pt: (b,) + (0,) * nd)

    sp, sn = pl.pallas_call(
        _sample_scores_kernel, name="sample_scores",
        grid_spec=pltpu.PrefetchScalarGridSpec(
            num_scalar_prefetch=1, grid=(nb, steps),
            in_specs=[per_b((IDX_HEADS * TQ, IDX_DIM)), per_b((IDX_HEADS * TQ, LANES)), per_b((PAGE_SIZE, IDX_DIM))]
            + [page_spec(IDX_DIM, j) for j in range(PAGES_PER_STEP)],
            out_specs=[pl.BlockSpec((1, TQ, kc), lambda b, p, pt: (b, 0, p)), per_b((TQ, LANES))],
            scratch_shapes=[pltpu.VMEM((kc, IDX_DIM), BF16)]),
        out_shape=[jax.ShapeDtypeStruct((nb, TQ, n_past), F32), jax.ShapeDtypeStruct((nb, TQ, LANES), F32)],
        compiler_params=_cparams(("parallel", "arbitrary")),
    )(page_table, qi_r, wexp, ki_new, *([cache_ik] * PAGES_PER_STEP))

    bt = Q_BLOCK // TQ
    ncol = n_past // LANES + 1
    cw = LANES * max(d for d in range(1, 9) if ncol % d == 0)
    assert nb % bt == 0
    thr, cut = pl.pallas_call(
        functools.partial(_sample_thr_kernel, n_past=n_past, ts=ts, topk=topk, cw=cw),
        grid=(nb // bt,), name="sample_threshold",
        in_specs=[pl.BlockSpec((bt, TQ, n_past), lambda i: (i, 0, 0)), pl.BlockSpec((bt, TQ, LANES), lambda i: (i, 0, 0))],
        out_specs=[pl.BlockSpec((bt, TQ, LANES), lambda i: (i, 0, 0))] * 2,
        out_shape=[jax.ShapeDtypeStruct((nb, TQ, LANES), I32)] * 2,
        scratch_shapes=[pltpu.VMEM((bt * TQ, n_past + LANES), I32)],
        compiler_params=_cparams(("parallel",)),
    )(sp, sn)

    o = pl.pallas_call(
        functools.partial(_sample_attn_kernel, ts=ts, n_past=n_past), name="sample_attn",
        grid_spec=pltpu.PrefetchScalarGridSpec(
            num_scalar_prefetch=1, grid=(nb, steps),
            in_specs=[per_b((N_HEADS * TQ, HEAD_DIM)), pl.BlockSpec((1, TQ, kc), lambda b, p, pt: (b, 0, p)),
                      per_b((TQ, LANES)), per_b((TQ, LANES)), per_b((TQ, LANES)),
                      per_b((PAGE_SIZE, KV_W)), per_b((PAGE_SIZE, KV_W))]
            + [page_spec(KV_W, j) for j in range(PAGES_PER_STEP)] * 2,
            out_specs=per_b((N_HEADS * TQ, HEAD_DIM)),
            scratch_shapes=[pltpu.VMEM((kc, KV_W), BF16), pltpu.VMEM((kc, KV_W), BF16),
                            pltpu.VMEM((N_HEADS * TQ, 1), F32), pltpu.VMEM((N_HEADS * TQ, 1), F32),
                            pltpu.VMEM((N_HEADS * TQ, HEAD_DIM), F32)]),
        out_shape=jax.ShapeDtypeStruct((nb, N_HEADS * TQ, HEAD_DIM), BF16),
        compiler_params=_cparams(("parallel", "arbitrary")),
    )(page_table, q_r, sp, sn, thr, cut, k_new, v_new, *([ck] * PAGES_PER_STEP), *([cv] * PAGES_PER_STEP))
    o = o.reshape(nb, N_HEADS, TQ, HEAD_DIM)[:, :, :ts].transpose(0, 2, 1, 3)
    return o.reshape(1, nb * ts, ATTN_W)


def _ssd_kernel(xbc_ref, z_ref, dt_ref, dtt_ref, cst_ref, sst_ref, cw_ref, cb_ref, alr_ref, alc_ref, dsk_ref,
                gn_ref, tri_ref, yn_ref, so_ref, xbuf, dtbuf, dttbuf, zbuf, ybuf, state, *, tin):
    ell = SSD_CHUNK
    c = pl.program_id(1)

    @pl.when(c == 0)
    def _():
        xbuf[0:SUBLANES, :] = cst_ref[0]
        state[...] = sst_ref[0]
        if tin < ell:
            xbuf[SUBLANES:, :] = jnp.zeros((ell, CONV_DIM), F32)
            dtbuf[...] = jnp.zeros(dtbuf.shape, F32)
            dttbuf[...] = jnp.zeros(dttbuf.shape, F32)
            zbuf[...] = jnp.zeros(zbuf.shape, F32)

    if tin < ell:
        xbuf[SUBLANES:SUBLANES + tin, :] = xbc_ref[0]
        dtbuf[0:tin, :] = dt_ref[0]
        dttbuf[:, 0:tin] = dtt_ref[0]
        zbuf[0:tin, :] = z_ref[0]
        dt, dtt, z = dtbuf[...], dttbuf[...], zbuf[...]
    else:
        xbuf[SUBLANES:, :] = xbc_ref[0]
        dt, dtt, z = dt_ref[0], dtt_ref[0], z_ref[0]

    conv = cb_ref[...]
    for j in range(CONV_W):
        conv = conv + xbuf[pl.ds(SUBLANES - (CONV_W - 1) + j, ell), :] * cw_ref[j:j + 1, :]
    act = _silu(conv)
    xbuf[0:SUBLANES, :] = xbuf[ell:ell + SUBLANES, :]

    tri = tri_ref[...]
    a_cum = jnp.dot(tri, dt * (-jnp.exp(alr_ref[...])), precision=lax.Precision.HIGHEST,
                    preferred_element_type=F32)
    a_cum_t = lax.dot_general(dtt * (-jnp.exp(alc_ref[...])), tri, (((1,), (1,)), ((), ())),
                              precision=lax.Precision.HIGHEST, preferred_element_type=F32)
    causal = tri > 0.5
    for g in range(SSM_GROUPS):
        bg = act[:, D_INNER + g * D_STATE:D_INNER + (g + 1) * D_STATE].astype(BF16)
        cg = act[:, D_INNER + (SSM_GROUPS + g) * D_STATE:D_INNER + (SSM_GROUPS + g + 1) * D_STATE].astype(BF16)
        cbm = _dot_nt(cg, bg)
        for r in range(HEADS_PER_GROUP):
            hh = g * HEADS_PER_GROUP + r
            xh = act[:, hh * SSM_HEAD_DIM:(hh + 1) * SSM_HEAD_DIM]
            acol = a_cum[:, hh:hh + 1]
            arow = a_cum_t[hh:hh + 1, :]
            alast = arow[:, ell - 1:ell]
            decay = jnp.exp(jnp.where(causal, acol - arow, -jnp.inf))
            xd = xh * dt[:, hh:hh + 1]
            sh = state[hh]
            y = _dot((cbm * decay).astype(BF16), xd.astype(BF16))
            y = y + _dot_nt(cg, sh.astype(BF16)) * jnp.exp(acol)
            y = y + dsk_ref[:, hh:hh + 1] * xh
            state[hh] = sh * jnp.exp(alast) + _dot_tn((xd * jnp.exp(alast - acol)).astype(BF16), bg)
            ybuf[:, hh * SSM_HEAD_DIM:(hh + 1) * SSM_HEAD_DIM] = y

    y = ybuf[...] * _silu(z)
    gw = D_INNER // SSM_GROUPS
    for g in range(SSM_GROUPS):
        cs = slice(g * gw, (g + 1) * gw)
        yn = (_rms(y[:, cs]) * gn_ref[:, cs]).astype(yn_ref.dtype)
        yn_ref[0, :, cs] = yn[:tin] if tin < ell else yn

    @pl.when(c == pl.num_programs(1) - 1)
    def _():
        so_ref[0] = state[...]


def _ssd(xbc, z, dt, conv_state, ssm_state, conv_w, conv_b, a_log, d_skip, g_norm):
    bsz, t, _ = xbc.shape
    ell = SSD_CHUNK
    tin = min(t, ell)
    nc = max(t // ell, 1)
    assert t == tin or t % ell == 0
    dtt = jnp.swapaxes(dt, 1, 2)
    cst = jnp.pad(conv_state, ((0, 0), (SUBLANES - (CONV_W - 1), 0), (0, 0)))
    pad_h = ((0, 0), (0, LANES - SSM_HEADS))
    alr = jnp.pad(a_log.reshape(1, SSM_HEADS), pad_h)
    alc = alr.reshape(LANES, 1)
    dsk = jnp.pad(d_skip.reshape(1, SSM_HEADS), pad_h)
    tri = jnp.tril(jnp.ones((ell, ell), F32))
    tok = lambda w: pl.BlockSpec((1, tin, w), lambda b, c: (b, c, 0))
    st_spec = pl.BlockSpec((1, SSM_HEADS, SSM_HEAD_DIM, D_STATE), lambda b, c: (b, 0, 0, 0))
    yn, s_out = pl.pallas_call(
        functools.partial(_ssd_kernel, tin=tin), grid=(bsz, nc), name="ssd",
        in_specs=[tok(CONV_DIM), tok(D_INNER), tok(LANES), pl.BlockSpec((1, LANES, tin), lambda b, c: (b, 0, c)),
                  pl.BlockSpec((1, SUBLANES, CONV_DIM), lambda b, c: (b, 0, 0)), st_spec,
                  _const_spec((CONV_W, CONV_DIM)), _const_spec((1, CONV_DIM)), _const_spec((1, LANES)),
                  _const_spec((LANES, 1)), _const_spec((1, LANES)), _const_spec((1, D_INNER)), _const_spec((ell, ell))],
        out_specs=[tok(D_INNER), st_spec],
        out_shape=[jax.ShapeDtypeStruct((bsz, t, D_INNER), BF16),
                   jax.ShapeDtypeStruct((bsz, SSM_HEADS, SSM_HEAD_DIM, D_STATE), F32)],
        scratch_shapes=[pltpu.VMEM((ell + SUBLANES, CONV_DIM), F32), pltpu.VMEM((ell, LANES), F32),
                        pltpu.VMEM((LANES, ell), F32), pltpu.VMEM((ell, D_INNER), F32),
                        pltpu.VMEM((ell, D_INNER), F32), pltpu.VMEM((SSM_HEADS, SSM_HEAD_DIM, D_STATE), F32)],
        compiler_params=_cparams(("parallel", "arbitrary")),
    )(xbc, z, dt, dtt, cst, ssm_state, conv_w, conv_b.reshape(1, CONV_DIM), alr, alc, dsk,
      g_norm.reshape(1, D_INNER), tri)
    return yn, s_out


def _mix_kernel(x_ref, at_ref, yn_ref, ga_ref, gm_ref, wa_ref, ws_ref, wo_ref, o_ref):
    mix = ga_ref[0] * _dot(at_ref[0], wa_ref[...]) + gm_ref[0] * _dot(yn_ref[0], ws_ref[...])
    o_ref[0] = x_ref[0] + _dot(mix.astype(BF16), wo_ref[...])


def _mix(x, attn, yn, sga, sgm, w_attn_out, w_ssm_out, w_o, tm):
    bsz, t, _ = x.shape
    tok = lambda w: pl.BlockSpec((1, tm, w), lambda b, i: (b, i, 0))
    return pl.pallas_call(
        _mix_kernel, grid=(bsz, t // tm), name="mix",
        in_specs=[tok(D_MODEL), tok(ATTN_W), tok(D_INNER), tok(D_MODEL), tok(D_MODEL),
                  _const_spec((ATTN_W, D_MODEL)), _const_spec((D_INNER, D_MODEL)), _const_spec((D_MODEL, D_MODEL))],
        out_specs=tok(D_MODEL),
        out_shape=jax.ShapeDtypeStruct((bsz, t, D_MODEL), F32),
        compiler_params=_cparams(("parallel", "parallel")),
    )(x, attn, yn, sga, sgm, w_attn_out, w_ssm_out, w_o)


def _route(logits):
    lane = lax.broadcasted_iota(I32, logits.shape, 1)
    ninf = -jnp.inf
    big = jnp.int32(1 << 20)

    def first_max(mask):
        mx = jnp.max(jnp.where(mask, logits, ninf), axis=1, keepdims=True)
        idx = jnp.min(jnp.where(jnp.logical_and(mask, logits == mx), lane, big), axis=1, keepdims=True)
        return mx, idx

    gmask = lane < N_EXPERT_GROUPS
    gmax, gsel = first_max(gmask)
    g_prob = 1.0 / jnp.sum(jnp.where(gmask, jnp.exp(logits - gmax), 0.0), axis=1, keepdims=True)
    lo = N_EXPERT_GROUPS + gsel * EXPERTS_PER_GROUP
    emask = jnp.logical_and(lane >= lo, lane < lo + EXPERTS_PER_GROUP)
    m1, i1 = first_max(emask)
    m2, i2 = first_max(jnp.logical_and(emask, lane != i1))
    zsum = jnp.sum(jnp.where(emask, jnp.exp(logits - m1), 0.0), axis=1, keepdims=True)
    p1 = 1.0 / zsum
    p2 = jnp.exp(m2 - m1) / zsum
    w1 = p1 / (p1 + p2) * g_prob
    w2 = p2 / (p1 + p2) * g_prob
    return jnp.where(lane == i1, w1, 0.0) + jnp.where(lane == i2, w2, 0.0)


def _moe_kernel(x_ref, pe_ref, gf_ref, wr_ref, br_ref, wg_ref, wu_ref, wd_ref, wp_ref, gp_ref, wpg_ref, gfin_ref,
                y_ref, hb_ref, gate_ref, acc_ref, *, final_norm):
    e = pl.program_id(2)

    @pl.when(e == 0)
    def _():
        h = _rms(x_ref[0]) * gf_ref[...]
        hb_ref[...] = h.astype(BF16)
        logits = jnp.dot(h, wr_ref[...], precision=lax.Precision.HIGHEST, preferred_element_type=F32) + br_ref[...]
        gate_ref[...] = _route(logits)
        acc_ref[...] = jnp.zeros(acc_ref.shape, F32)

    hb = hb_ref[...]
    act = _silu(_dot(hb, wg_ref[0])) * _dot(hb, wu_ref[0])
    lane = lax.broadcasted_iota(I32, gate_ref.shape, 1)
    ge = jnp.sum(jnp.where(lane == e + N_EXPERT_GROUPS, gate_ref[...], 0.0), axis=1, keepdims=True)
    acc_ref[...] += ge * _dot(act.astype(BF16), wd_ref[0])

    @pl.when(e == pl.num_programs(2) - 1)
    def _():
        x2 = x_ref[0] + acc_ref[...]
        gate = jax.nn.sigmoid(_dot((_rms(x2) * gp_ref[...]).astype(BF16), wpg_ref[...]))
        x3 = x2 + _dot(pe_ref[0].astype(BF16), wp_ref[...]) * gate
        y_ref[0] = _rms(x3) * gfin_ref[...] if final_norm else x3


def _moe(x, p_emb, g_ffn, w_router, b_router, w_gate, w_up, w_down, w_ple, g_ple, w_ple_gate, g_final, tm,
         final_norm):
    bsz, t, _ = x.shape
    tok = lambda w: pl.BlockSpec((1, tm, w), lambda b, i, e: (b, i, 0))
    cst = _const_spec
    return pl.pallas_call(
        functools.partial(_moe_kernel, final_norm=final_norm), grid=(bsz, t // tm, N_EXPERTS), name="moe",
        in_specs=[tok(D_MODEL), tok(PLE_DIM), cst((1, D_MODEL)), cst((D_MODEL, LANES)), cst((1, LANES)),
                  pl.BlockSpec((1, D_MODEL, EXPERT_FF), lambda b, i, e: (e, 0, 0)),
                  pl.BlockSpec((1, D_MODEL, EXPERT_FF), lambda b, i, e: (e, 0, 0)),
                  pl.BlockSpec((1, EXPERT_FF, D_MODEL), lambda b, i, e: (e, 0, 0)),
                  cst((PLE_DIM, D_MODEL)), cst((1, D_MODEL)), cst((D_MODEL, D_MODEL)), cst((1, D_MODEL))],
        out_specs=tok(D_MODEL),
        out_shape=jax.ShapeDtypeStruct((bsz, t, D_MODEL), F32),
        scratch_shapes=[pltpu.VMEM((tm, D_MODEL), BF16), pltpu.VMEM((tm, LANES), F32), pltpu.VMEM((tm, D_MODEL), F32)],
        compiler_params=_cparams(("parallel", "parallel", "arbitrary")),
    )(x, p_emb, g_ffn, w_router, b_router, w_gate, w_up, w_down, w_ple, g_ple, w_ple_gate, g_final)


def _rope_tables(pos):
    posf = pos.astype(F32)[:, None]
    n = pos.shape[0]

    def cs(rot):
        half = rot // 2
        inv = ROPE_THETA ** (-(jnp.arange(half, dtype=F32) * 2.0 / rot))
        ang = posf * inv[None, :]
        return jnp.cos(ang), jnp.sin(ang), half

    def pattern(width, rot):
        cos, sin, half = cs(rot)
        ones = jnp.ones((n, width - rot), F32)
        zeros = jnp.zeros((n, width - rot), F32)
        zh = jnp.zeros((n, half), F32)
        return (jnp.concatenate([cos, cos, ones], 1), jnp.concatenate([-sin, zh, zeros], 1),
                jnp.concatenate([zh, sin, zeros], 1))

    c128, s1_128, s2_128 = pattern(HEAD_DIM, HEAD_DIM // 4)
    c64, s1_64, s2_64 = pattern(IDX_DIM, IDX_DIM // 4)
    two = lambda a: jnp.concatenate([a, a], 1)
    wscale = jnp.full((n, IDX_HEADS), (IDX_HEADS * IDX_DIM) ** -0.5, F32)
    rest1 = jnp.ones((n, LANES - IDX_DIM - IDX_HEADS), F32)
    z64 = jnp.zeros((n, LANES - IDX_DIM), F32)
    return jnp.stack([c128, s1_128, s2_128, two(c64), two(s1_64), two(s2_64),
                      jnp.concatenate([c64, wscale, rest1], 1), jnp.concatenate([s1_64, z64], 1),
                      jnp.concatenate([s2_64, z64], 1)])


def _relayout_w_in(w_in, dt_bias):
    offs = [int(o) for o in np.cumsum(IN_SPLITS)[:-1]]
    wq, wk, wv, wqi, wki, wwi, wz, wxbc, wdt, wga, wgm = jnp.split(w_in, offs, axis=1)
    padc = lambda a, n: jnp.pad(a, ((0, 0), (0, n - a.shape[1])))
    w_a = jnp.concatenate([wq, wk, wv, wqi, padc(jnp.concatenate([wki, wwi], 1), LANES)], 1).astype(BF16)
    w_b = jnp.concatenate([wz, wxbc, padc(wdt, LANES)], 1).astype(BF16)
    w_c = jnp.concatenate([wga, wgm], 1).astype(BF16)
    dtb = jnp.pad(dt_bias.reshape(1, SSM_HEADS), ((0, 0), (0, LANES - SSM_HEADS)))
    return w_a, w_b, w_c, dtb


def kernel(x_prompt, x_sample, cache_k, cache_v, cache_idx_k, state_ssm, state_conv, page_table, p_prompt, p_sample, g_mix, w_in, w_attn_out, conv_w, conv_b, dt_bias, a_log, d_skip, g_ssm_norm, w_ssm_out, w_o, g_ffn, w_group_router, b_group_router, w_expert_router, b_expert_router, w_expert_gate, w_expert_up, w_expert_down, w_ple, g_ple, w_ple_gate, g_final):
    depth = w_in.shape[0]
    bp, tp, _ = x_prompt.shape
    nb, ts, _ = x_sample.shape
    past = page_table.shape[1] * PAGE_SIZE
    row = lambda a: a.reshape(1, -1)
    tabs_p = _rope_tables(jnp.arange(tp))
    tabs_s = _rope_tables(past + (jnp.arange(nb * ts) % ts))
    conv0 = jnp.zeros((bp, CONV_W - 1, CONV_DIM), F32)
    ssm0 = jnp.zeros((bp, SSM_HEADS, SSM_HEAD_DIM, D_STATE), F32)
    xp = x_prompt
    xs = x_sample.reshape(1, nb * ts, D_MODEL)
    outs = [[] for _ in range(10)]
    for i in range(depth):
        w_a, w_b, w_c, dtb = _relayout_w_in(w_in[i], dt_bias[i])
        wao, wso, wo = w_attn_out[i].astype(BF16), w_ssm_out[i].astype(BF16), w_o[i].astype(BF16)
        w_router = jnp.pad(jnp.concatenate([w_group_router[i], w_expert_router[i]], 1),
                           ((0, 0), (0, LANES - N_EXPERT_GROUPS - N_EXPERTS)))
        b_router = jnp.pad(jnp.concatenate([b_group_router[i], b_expert_router[i]]).reshape(1, -1),
                           ((0, 0), (0, LANES - N_EXPERT_GROUPS - N_EXPERTS)))
        wg, wu, wd = w_expert_gate[i].astype(BF16), w_expert_up[i].astype(BF16), w_expert_down[i].astype(BF16)
        wple, wpleg = w_ple[i].astype(BF16), w_ple_gate[i].astype(BF16)

        last_layer = i == depth - 1

        def layer(x, p_emb, tabs, attn_fn, conv_state, ssm_state, batch_shape, tm, tm_moe):
            (q_hm, k_f, v_f, k_bf, v_bf, qi_hm, kiw, ki_bf), (z, xbc, dt), (sga, sgm) = _in_proj(
                x, row(g_mix[i]), w_a, w_b, w_c, tabs, dtb, tm)
            attn = attn_fn(q_hm, qi_hm, kiw, ki_bf, k_f, v_f, k_bf, v_bf)
            sb, st = batch_shape
            seq = lambda a: a.reshape(sb, st, a.shape[-1])
            yn, s_new = _ssd(seq(xbc), seq(z), seq(dt), conv_state, ssm_state, conv_w[i], conv_b[i], a_log[i],
                             d_skip[i], g_ssm_norm[i])
            x1 = _mix(x, attn, yn.reshape(x.shape[0], x.shape[1], D_INNER), sga, sgm, wao, wso, wo, tm)
            x2 = _moe(x1, p_emb, row(g_ffn[i]), w_router, b_router, wg, wu, wd, wple, row(g_ple[i]), wpleg,
                      row(g_final), tm_moe, last_layer)
            assert st >= CONV_W - 1
            conv_new = seq(xbc)[:, st - (CONV_W - 1):]
            return x2, k_f, v_f, kiw[..., :IDX_DIM], conv_new, s_new

        attn_p = lambda q_hm, qi_hm, kiw, ki_bf, k_f, v_f, k_bf, v_bf: _attn_prompt(q_hm, qi_hm, kiw, ki_bf, k_bf, v_bf)
        attn_s = lambda q_hm, qi_hm, kiw, ki_bf, k_f, v_f, k_bf, v_bf: _attn_sample(
            q_hm, qi_hm, kiw, k_f, v_f, cache_k[i], cache_v[i], cache_idx_k[i], page_table)
        xp, kp, vp, ikp, convp, ssmp = layer(xp, p_prompt[i], tabs_p, attn_p, conv0, ssm0, (bp, tp), 512, 1024)
        xs, ks, vs, iks, convs, ssms = layer(xs, p_sample[i].reshape(1, nb * ts, PLE_DIM), tabs_s, attn_s,
                                             state_conv[i], state_ssm[i], (nb, ts), nb * ts, nb * ts)
        vals = [kp.reshape(bp, tp, KV_HEADS, HEAD_DIM), vp.reshape(bp, tp, KV_HEADS, HEAD_DIM), ikp,
                ks.reshape(nb, ts, KV_HEADS, HEAD_DIM), vs.reshape(nb, ts, KV_HEADS, HEAD_DIM),
                iks.reshape(nb, ts, IDX_DIM), ssmp, convp, ssms, convs]
        for lst, v in zip(outs, vals):
            lst.append(v)
    return (xp, xs.reshape(nb, ts, D_MODEL)) + tuple(jnp.stack(lst) for lst in outs)
```

```python
import functools

import jax
import jax.numpy as jnp
import numpy as np
from jax import lax
from jax.experimental import pallas as pl
from jax.experimental.pallas import tpu as pltpu

F32 = jnp.float32
BF16 = jnp.bfloat16
I32 = jnp.int32

D_MODEL = 1024
N_HEADS = 8
HEAD_DIM = 128
KV_HEADS = 2
Q_PER_KV = N_HEADS // KV_HEADS
ATTN_W = N_HEADS * HEAD_DIM
KV_W = KV_HEADS * HEAD_DIM
ROPE_THETA = 500000.0
IDX_HEADS = 8
IDX_DIM = 64
IDX_W = IDX_HEADS * IDX_DIM
TOPK_MAX = 256
Q_BLOCK = 128
PAGE_SIZE = 128
D_INNER = 2 * D_MODEL
SSM_HEAD_DIM = 64
SSM_HEADS = D_INNER // SSM_HEAD_DIM
SSM_GROUPS = 4
HEADS_PER_GROUP = SSM_HEADS // SSM_GROUPS
D_STATE = 128
CONV_W = 4
CONV_DIM = D_INNER + 2 * SSM_GROUPS * D_STATE
SSD_CHUNK = 128
SSD_SHORT_CHUNK = 16
N_EXPERT_GROUPS = 4
EXPERTS_PER_GROUP = 8
N_EXPERTS = N_EXPERT_GROUPS * EXPERTS_PER_GROUP
EXPERT_FF = 256
PLE_DIM = 256
EPS = 1e-6
IN_SPLITS = (ATTN_W, KV_W, KV_W, IDX_W, IDX_DIM, IDX_HEADS, D_INNER, CONV_DIM, SSM_HEADS, D_MODEL, D_MODEL)

LANES = 128
SUBLANES = 8
VMEM_LIMIT = 56 * 1024 * 1024
NEG_BIG = -1e30
CUT_ALL = 1 << 20
PAGES_PER_STEP = 32

A_Q, A_K, A_V, A_QI, A_KIW = 0, ATTN_W, ATTN_W + KV_W, ATTN_W + 2 * KV_W, ATTN_W + 2 * KV_W + IDX_W
A_COLS = A_KIW + LANES
B_Z, B_XBC, B_DT = 0, D_INNER, D_INNER + CONV_DIM
B_COLS = B_DT + LANES


def _cparams(sem):
    return pltpu.CompilerParams(dimension_semantics=sem, vmem_limit_bytes=VMEM_LIMIT)


def _rms(x):
    return x * lax.rsqrt(jnp.mean(x * x, axis=-1, keepdims=True) + EPS)


def _dot(a, b):
    return jnp.dot(a, b, preferred_element_type=F32)


def _dot_nt(a, b):
    return lax.dot_general(a, b, (((1,), (1,)), ((), ())), preferred_element_type=F32)


def _dot_tn(a, b):
    return lax.dot_general(a, b, (((0,), (0,)), ((), ())), preferred_element_type=F32)


def _silu(x):
    return x * jax.nn.sigmoid(x)


def _rope(y, tab_ref, base, shift):
    return (y * tab_ref[base] + pltpu.roll(y, LANES - shift, 1) * tab_ref[base + 1]
            + pltpu.roll(y, shift, 1) * tab_ref[base + 2])


def _proj_attn_kernel(x_ref, g_ref, w_ref, tab_ref, q_ref, kf_ref, vf_ref, kb_ref, vb_ref, qi_ref, kiw_ref, kib_ref):
    h = (_rms(x_ref[0]) * g_ref[...]).astype(BF16)
    yq = _dot(h, w_ref[:, A_Q:A_K])
    for hh in range(N_HEADS):
        q_ref[0, hh] = _rope(yq[:, hh * LANES:(hh + 1) * LANES], tab_ref, 0, 16).astype(BF16)
    ykv = _dot(h, w_ref[:, A_K:A_QI])
    for hh in range(KV_HEADS):
        kr = _rope(ykv[:, hh * LANES:(hh + 1) * LANES], tab_ref, 0, 16)
        kf_ref[0, :, hh * LANES:(hh + 1) * LANES] = kr
        kb_ref[0, :, hh * LANES:(hh + 1) * LANES] = kr.astype(BF16)
    v = ykv[:, KV_W:]
    vf_ref[0] = v
    ones = jnp.ones((v.shape[0], HEAD_DIM), BF16)
    for hh in range(KV_HEADS):
        vb_ref[0, :, 2 * hh * HEAD_DIM:(2 * hh + 1) * HEAD_DIM] = v[:, hh * HEAD_DIM:(hh + 1) * HEAD_DIM].astype(BF16)
        vb_ref[0, :, (2 * hh + 1) * HEAD_DIM:(2 * hh + 2) * HEAD_DIM] = ones
    yi = _dot(h, w_ref[:, A_QI:A_COLS])
    for c in range(IDX_W // LANES):
        r = _rope(yi[:, c * LANES:(c + 1) * LANES], tab_ref, 3, 8).astype(BF16)
        qi_ref[0, 2 * c] = r[:, :IDX_DIM]
        qi_ref[0, 2 * c + 1] = r[:, IDX_DIM:]
    kiw = _rope(yi[:, IDX_W:], tab_ref, 6, 8)
    kiw_ref[0] = kiw
    kib_ref[0] = kiw[:, :IDX_DIM].astype(BF16)


def _proj_ssm_kernel(x_ref, g_ref, w_ref, dtb_ref, z_ref, xbc_ref, dt_ref):
    h = (_rms(x_ref[0]) * g_ref[...]).astype(BF16)
    z_ref[0] = _dot(h, w_ref[:, B_Z:B_XBC])
    xbc_ref[0] = _dot(h, w_ref[:, B_XBC:B_DT])
    dt_ref[0] = jax.nn.softplus(_dot(h, w_ref[:, B_DT:B_COLS]) + dtb_ref[...])


def _proj_gate_kernel(x_ref, g_ref, w_ref, ga_ref, gm_ref):
    h = (_rms(x_ref[0]) * g_ref[...]).astype(BF16)
    y = _dot(h, w_ref[...])
    ga_ref[0] = jax.nn.sigmoid(y[:, :D_MODEL])
    gm_ref[0] = jax.nn.sigmoid(y[:, D_MODEL:])


def _const_spec(shape):
    nd = len(shape)
    return pl.BlockSpec(shape, lambda *_: (0,) * nd, pipeline_mode=pl.Buffered(1))


def _in_proj(x, g_mix, w_a, w_b, w_c, tabs, dtb, tm):
    bsz, t, _ = x.shape
    grid = (bsz, t // tm)
    xs = pl.BlockSpec((1, tm, D_MODEL), lambda b, i: (b, i, 0))
    gs = _const_spec((1, D_MODEL))

    def tok(width):
        return pl.BlockSpec((1, tm, width), lambda b, i: (b, i, 0))

    def hm(nh, width):
        return pl.BlockSpec((1, nh, tm, width), lambda b, i: (b, 0, i, 0))

    sd = jax.ShapeDtypeStruct
    attn = pl.pallas_call(
        _proj_attn_kernel, grid=grid, name="proj_attn",
        in_specs=[xs, gs, _const_spec((D_MODEL, A_COLS)), pl.BlockSpec((9, tm, LANES), lambda b, i: (0, i, 0))],
        out_specs=[hm(N_HEADS, HEAD_DIM), tok(KV_W), tok(KV_W), tok(KV_W), tok(2 * KV_W), hm(IDX_HEADS, IDX_DIM),
                   tok(LANES), tok(IDX_DIM)],
        out_shape=[sd((bsz, N_HEADS, t, HEAD_DIM), BF16), sd((bsz, t, KV_W), F32), sd((bsz, t, KV_W), F32),
                   sd((bsz, t, KV_W), BF16), sd((bsz, t, 2 * KV_W), BF16), sd((bsz, IDX_HEADS, t, IDX_DIM), BF16),
                   sd((bsz, t, LANES), F32), sd((bsz, t, IDX_DIM), BF16)],
        compiler_params=_cparams(("parallel", "parallel")),
    )(x, g_mix, w_a, tabs)
    ssm = pl.pallas_call(
        _proj_ssm_kernel, grid=grid, name="proj_ssm",
        in_specs=[xs, gs, _const_spec((D_MODEL, B_COLS)), _const_spec((1, LANES))],
        out_specs=[tok(D_INNER), tok(CONV_DIM), tok(LANES)],
        out_shape=[sd((bsz, t, D_INNER), F32), sd((bsz, t, CONV_DIM), F32), sd((bsz, t, LANES), F32)],
        compiler_params=_cparams(("parallel", "parallel")),
    )(x, g_mix, w_b, dtb)
    gates = pl.pallas_call(
        _proj_gate_kernel, grid=grid, name="proj_gate",
        in_specs=[xs, gs, _const_spec((D_MODEL, 2 * D_MODEL))],
        out_specs=[tok(D_MODEL), tok(D_MODEL)],
        out_shape=[sd((bsz, t, D_MODEL), F32), sd((bsz, t, D_MODEL), F32)],
        compiler_params=_cparams(("parallel", "parallel")),
    )(x, g_mix, w_c)
    return attn, ssm, gates


_LANE_REDUCE = {"sum": jnp.sum, "min": jnp.min, "max": jnp.max}


def _scan(sc_ref, nchunk, cw, fn, inits, kinds):
    def body(c, accs):
        off = pl.multiple_of(c * cw, cw)
        for j in range(cw // LANES):
            accs = fn(sc_ref[:, pl.ds(off + j * LANES, LANES)], off + j * LANES, accs)
        return accs

    accs = lax.fori_loop(0, nchunk, body, tuple(inits))
    return tuple(_LANE_REDUCE[k](a, axis=1, keepdims=True) for a, k in zip(accs, kinds))


SEARCH_MAX_ITERS = 96
SEARCH_STALL_ITERS = 6


def _select_threshold(sc_ref, nchunk, cw, topk):
    rows = sc_ref.shape[0]
    inf = jnp.inf
    zeros = jnp.zeros((rows, LANES), F32)
    pinf = jnp.full((rows, LANES), inf, F32)

    def bcast(v):
        return jnp.broadcast_to(v, (rows, LANES))

    def any_row(mask):
        return (jnp.max(jnp.where(mask, 1.0, 0.0)) > 0.0).astype(I32)

    def stats(t, col, acc):
        valid = t > -inf
        return (jnp.minimum(acc[0], jnp.where(valid, t, inf)), jnp.maximum(acc[1], t),
                acc[2] + jnp.where(valid, 1.0, 0.0))

    vmin, vmax, nval = _scan(sc_ref, nchunk, cw, stats, (pinf, -pinf, zeros), ("min", "max", "sum"))
    kk = jnp.minimum(float(topk), nval)
    kk1 = jnp.maximum(kk, 1.0)

    def count_ge(m):
        mb = bcast(m)
        return _scan(sc_ref, nchunk, cw, lambda t, col, acc: (acc[0] + jnp.where(t >= mb, 1.0, 0.0),),
                     (zeros,), ("sum",))[0]

    def search_body(st):
        it, _, lo, hi, clo, chi, stall = st
        ratio = jnp.log(jnp.maximum(clo, 1.0) / kk1) / jnp.log(jnp.maximum(clo, 1.0) / jnp.maximum(chi, 0.5))
        f = jnp.where(it % 2 == 0, jnp.clip(ratio, 0.02, 0.98), 0.5)
        m = lo * (1.0 - f) + hi * f
        c = count_ge(m)
        ge = c >= kk
        changed = c != jnp.where(ge, clo, chi)
        lo, clo = jnp.where(ge, m, lo), jnp.where(ge, c, clo)
        hi, chi = jnp.where(ge, hi, m), jnp.where(ge, chi, c)
        stall = jnp.where(changed, 0, stall + 1)
        go = any_row(jnp.logical_and(clo > kk, stall < SEARCH_STALL_ITERS))
        return it + 1, go, lo, hi, clo, chi, stall

    st0 = (jnp.int32(0), any_row(nval > kk), jnp.where(kk >= 1.0, vmin, 0.0), jnp.where(kk >= 1.0, vmax, 0.0), nval,
           jnp.zeros((rows, 1), F32), jnp.zeros((rows, 1), I32))
    st = lax.while_loop(lambda st: jnp.logical_and(st[1] > 0, st[0] < SEARCH_MAX_ITERS), search_body, st0)
    lo_b = bcast(st[2])
    thr0 = _scan(sc_ref, nchunk, cw, lambda t, col, acc: (jnp.minimum(acc[0], jnp.where(t >= lo_b, t, inf)),),
                 (pinf,), ("min",))[0]

    def snap_body(st):
        _, thr, _ = st
        tb = bcast(thr)

        def fn(t, col, acc):
            gt = t > tb
            return acc[0] + jnp.where(gt, 1.0, 0.0), jnp.minimum(acc[1], jnp.where(gt, t, inf))

        n_gt, nxt = _scan(sc_ref, nchunk, cw, fn, (zeros, pinf), ("sum", "min"))
        low = jnp.logical_and(n_gt >= kk, kk >= 1.0)
        return any_row(low), jnp.where(low, nxt, thr), n_gt

    _, thr, n_gt = lax.while_loop(lambda st: st[0] > 0, snap_body, (jnp.int32(1), thr0, jnp.zeros((rows, 1), F32)))
    thr = jnp.where(kk >= 1.0, thr, inf)
    thr_b = bcast(thr)
    n_eq = _scan(sc_ref, nchunk, cw, lambda t, col, acc: (acc[0] + jnp.where(t == thr_b, 1.0, 0.0),),
                 (zeros,), ("sum",))[0]
    need = kk - n_gt
    tie = jnp.where(jnp.logical_and(n_eq > need, kk >= 1.0), 1.0, 0.0)
    lane = lax.broadcasted_iota(I32, (rows, LANES), 1)

    def tie_cut():
        def body(j, u):
            cand = u | (jnp.int32(1) << (13 - j))
            cb = bcast(cand)
            c = _scan(sc_ref, nchunk, cw,
                      lambda t, col, acc: (acc[0] + jnp.where(jnp.logical_and(t == thr_b, lane + col < cb), 1.0, 0.0),),
                      (zeros,), ("sum",))[0]
            return jnp.where(c < need, cand, u)

        u = lax.fori_loop(0, 14, body, jnp.zeros((rows, 1), I32))
        return jnp.where(tie > 0.5, u, CUT_ALL).astype(I32)

    cut = lax.cond(any_row(tie > 0.5) > 0, tie_cut, lambda: jnp.full((rows, 1), CUT_ALL, I32))
    return thr, cut


def _select_bias(s, col, thr_b, cut_b):
    tie = jnp.logical_and(s == thr_b, col <= cut_b)
    return jnp.where(jnp.logical_or(s > thr_b, tie), 0.0, NEG_BIG)


def _index_scores(qi, wexp, keys, qb, stacked, keys_t=False):
    ncol = keys.shape[1 if keys_t else 0] // LANES
    mm = _dot if keys_t else _dot_nt
    cols = [jnp.zeros((qb, LANES), F32)] * ncol
    if stacked:
        s_all = mm(qi, keys)
    for hh in range(IDX_HEADS):
        rs = slice(hh * qb, (hh + 1) * qb)
        s = s_all[rs] if stacked else mm(qi[rs], keys)
        w = wexp[rs]
        cols = [a + jnp.maximum(s[:, j * LANES:(j + 1) * LANES], 0.0) * w for j, a in enumerate(cols)]
    return cols[0] if ncol == 1 else jnp.concatenate(cols, axis=1)


def _attn_prompt_kernel(q_ref, qi_ref, kiw_ref, kib_ref, kb_ref, vx_ref, o_ref,
                        sc_ref, wexp_ref, mx_ref, acc_ref, *, kc, topk):
    i = pl.program_id(1)
    qb = Q_BLOCK
    nch = (i * qb) // kc + 1
    ntile = kc // LANES
    rows_g = Q_PER_KV * qb
    rowpos = i * qb + lax.broadcasted_iota(I32, (qb, kc), 0)
    lanecol = lax.broadcasted_iota(I32, (qb, kc), 1)

    kiw = kiw_ref[0]
    for hh in range(IDX_HEADS):
        wexp_ref[hh * qb:(hh + 1) * qb, :] = jnp.broadcast_to(kiw[:, IDX_DIM + hh:IDX_DIM + hh + 1], (qb, LANES))
    qi = qi_ref[0].reshape(IDX_HEADS * qb, IDX_DIM)

    def score_body(c, carry):
        off = pl.multiple_of(c * kc, kc)
        sc = _index_scores(qi, wexp_ref[...], kib_ref[0, pl.ds(off, kc), :], qb, stacked=False)
        sc_ref[:, pl.ds(off, kc)] = jnp.where(lanecol + off <= rowpos, sc, -jnp.inf)
        return carry

    lax.fori_loop(0, nch, score_body, 0)
    thr, cut = _select_threshold(sc_ref, nch, kc, topk)
    thr_b = jnp.broadcast_to(thr, (qb, kc))
    cut_b = jnp.broadcast_to(cut, (qb, kc))
    scale = HEAD_DIM ** -0.5

    def masked_logits(c):
        off = pl.multiple_of(c * kc, kc)
        bias = _select_bias(sc_ref[:, pl.ds(off, kc)], lanecol + off, thr_b, cut_b)
        s = [_dot_nt(q_ref[0, g * Q_PER_KV:(g + 1) * Q_PER_KV].reshape(rows_g, HEAD_DIM),
                     kb_ref[0, pl.ds(off, kc), g * HEAD_DIM:(g + 1) * HEAD_DIM]) for g in range(KV_HEADS)]
        return off, bias, s

    mx_ref[...] = jnp.full(mx_ref.shape, NEG_BIG, F32)

    def max_body(c, carry):
        _, bias, s = masked_logits(c)
        for hh in range(N_HEADS):
            g, r = divmod(hh, Q_PER_KV)
            t = s[g][r * qb:(r + 1) * qb] + bias
            mt = t[:, :LANES]
            for j in range(1, ntile):
                mt = jnp.maximum(mt, t[:, j * LANES:(j + 1) * LANES])
            rows = slice(hh * qb, (hh + 1) * qb)
            mx_ref[rows] = jnp.maximum(mx_ref[rows], mt)
        return carry

    lax.fori_loop(0, nch, max_body, 0)
    for hh in range(N_HEADS):
        rows = slice(hh * qb, (hh + 1) * qb)
        mx_ref[rows] = jnp.broadcast_to(jnp.max(mx_ref[rows], axis=1, keepdims=True), (qb, LANES))

    acc_ref[...] = jnp.zeros(acc_ref.shape, F32)

    def sum_body(c, carry):
        off, bias, s = masked_logits(c)
        for g in range(KV_HEADS):
            ps = []
            for r in range(Q_PER_KV):
                hh = g * Q_PER_KV + r
                mh = mx_ref[hh * qb:(hh + 1) * qb]
                t = s[g][r * qb:(r + 1) * qb] + bias
                ps.append(jnp.concatenate(
                    [jnp.exp((t[:, j * LANES:(j + 1) * LANES] - mh) * scale) for j in range(ntile)], axis=1).astype(BF16))
            acc_ref[g * rows_g:(g + 1) * rows_g] += _dot(
                jnp.concatenate(ps, axis=0), vx_ref[0, pl.ds(off, kc), 2 * g * HEAD_DIM:(2 * g + 2) * HEAD_DIM])
        return carry

    lax.fori_loop(0, nch, sum_body, 0)
    for hh in range(N_HEADS):
        a = acc_ref[hh * qb:(hh + 1) * qb]
        o_ref[0, :, hh * HEAD_DIM:(hh + 1) * HEAD_DIM] = (a[:, :HEAD_DIM] / a[:, HEAD_DIM:]).astype(o_ref.dtype)


def _attn_prompt(q_hm, qi_hm, kiw, ki_bf, k_bf, v_x, kc=512):
    bsz, _, t, _ = q_hm.shape
    topk = min(TOPK_MAX, t // 4)
    kc = min(kc, t)
    qb = Q_BLOCK
    kern = functools.partial(_attn_prompt_kernel, kc=kc, topk=topk)

    def per_batch(width):
        return pl.BlockSpec((1, t, width), lambda b, i: (b, 0, 0), pipeline_mode=pl.Buffered(1))

    return pl.pallas_call(
        kern, grid=(bsz, t // qb), name="attn_prompt",
        in_specs=[pl.BlockSpec((1, N_HEADS, qb, HEAD_DIM), lambda b, i: (b, 0, i, 0)),
                  pl.BlockSpec((1, IDX_HEADS, qb, IDX_DIM), lambda b, i: (b, 0, i, 0)),
                  pl.BlockSpec((1, qb, LANES), lambda b, i: (b, i, 0)),
                  per_batch(IDX_DIM), per_batch(KV_W), per_batch(2 * KV_W)],
        out_specs=pl.BlockSpec((1, qb, ATTN_W), lambda b, i: (b, i, 0)),
        out_shape=jax.ShapeDtypeStruct((bsz, t, ATTN_W), BF16),
        scratch_shapes=[pltpu.VMEM((qb, t), F32), pltpu.VMEM((IDX_HEADS * qb, LANES), F32),
                        pltpu.VMEM((N_HEADS * qb, LANES), F32), pltpu.VMEM((N_HEADS * qb, 2 * HEAD_DIM), F32)],
        compiler_params=_cparams(("parallel", "arbitrary")),
    )(q_hm, qi_hm, kiw, ki_bf, k_bf, v_x)


TQ = SUBLANES


def _sample_scores_kernel(pt_ref, qi_ref, wexp_ref, kin_ref, *rest):
    pages = rest[:PAGES_PER_STEP]
    sp_ref, sn_ref, kbuf = rest[PAGES_PER_STEP:]
    for j, pg in enumerate(pages):
        kbuf[:, j * PAGE_SIZE:(j + 1) * PAGE_SIZE] = pg[0].astype(BF16)
    sp_ref[0] = _index_scores(qi_ref[0], wexp_ref[0], kbuf[...], TQ, stacked=True, keys_t=True)

    @pl.when(pl.program_id(1) == pl.num_programs(1) - 1)
    def _():
        sn_ref[0] = _index_scores(qi_ref[0], wexp_ref[0], kin_ref[0], TQ, stacked=True, keys_t=True)


def _new_key_valid(rows, ts):
    trow = lax.broadcasted_iota(I32, (rows, LANES), 0) % TQ
    lane = lax.broadcasted_iota(I32, (rows, LANES), 1)
    return jnp.logical_and(lane <= trow, trow < ts)


def _sample_thr_kernel(sp_ref, sn_ref, thr_ref, cut_ref, sc_ref, *, n_past, ts, topk, cw):
    nb = sp_ref.shape[0]
    rows = nb * TQ
    live = lax.broadcasted_iota(I32, (rows, n_past), 0) % TQ < ts
    sc_ref[:, :n_past] = jnp.where(live, sp_ref[...].reshape(rows, n_past), -jnp.inf)
    sc_ref[:, n_past:] = jnp.where(_new_key_valid(rows, ts), sn_ref[...].reshape(rows, LANES), -jnp.inf)
    thr, cut = _select_threshold(sc_ref, (n_past + LANES) // cw, cw, topk)
    thr_ref[...] = jnp.broadcast_to(thr, (rows, LANES)).reshape(nb, TQ, LANES)
    cut_ref[...] = jnp.broadcast_to(cut, (rows, LANES)).reshape(nb, TQ, LANES)


def _sample_attn_kernel(pt_ref, q_ref, sp_ref, sn_ref, thr_ref, cut_ref, kn_ref, vn_ref, *rest, ts, n_past):
    kpages = rest[:PAGES_PER_STEP]
    vpages = rest[PAGES_PER_STEP:2 * PAGES_PER_STEP]
    o_ref, kbuf, vbuf, m_ref, acc_ref = rest[2 * PAGES_PER_STEP:]
    pg = pl.program_id(1)
    kc = PAGES_PER_STEP * PAGE_SIZE
    rows_g = Q_PER_KV * TQ
    scale = HEAD_DIM ** -0.5

    @pl.when(pg == 0)
    def _():
        m_ref[...] = jnp.full(m_ref.shape, NEG_BIG, F32)
        acc_ref[...] = jnp.zeros(acc_ref.shape, F32)
        vbuf[:, :, HEAD_DIM:] = jnp.ones((KV_HEADS, kc, HEAD_DIM), BF16)

    for j in range(PAGES_PER_STEP):
        for g in range(KV_HEADS):
            rs = pl.ds(g, PAGE_SIZE, stride=KV_HEADS)
            kbuf[g, j * PAGE_SIZE:(j + 1) * PAGE_SIZE, :] = kpages[j][0, rs, :].astype(BF16)
            vbuf[g, j * PAGE_SIZE:(j + 1) * PAGE_SIZE, :HEAD_DIM] = vpages[j][0, rs, :].astype(BF16)

    def attend(bias8, kfn, vfn):
        bias = jnp.concatenate([bias8] * Q_PER_KV, axis=0)
        for g in range(KV_HEADS):
            rows = slice(g * rows_g, (g + 1) * rows_g)
            t = _dot_nt(q_ref[0, rows], kfn(g)) + bias
            m_old = m_ref[rows]
            m_new = jnp.maximum(m_old, jnp.max(t, axis=1, keepdims=True))
            p = jnp.exp((t - m_new) * scale)
            acc_ref[rows] = jnp.exp((m_old - m_new) * scale) * acc_ref[rows] + _dot(p.astype(BF16), vfn(g))
            m_ref[rows] = m_new

    thr_b = jnp.concatenate([thr_ref[0]] * (kc // LANES), axis=1)
    cut_b = jnp.concatenate([cut_ref[0]] * (kc // LANES), axis=1)
    col = lax.broadcasted_iota(I32, (TQ, kc), 1) + pg * kc
    attend(_select_bias(sp_ref[0], col, thr_b, cut_b), lambda g: kbuf[g], lambda g: vbuf[g])

    @pl.when(pg == pl.num_programs(1) - 1)
    def _():
        lane = lax.broadcasted_iota(I32, (TQ, LANES), 1)
        s_new = jnp.where(_new_key_valid(TQ, ts), sn_ref[0], -jnp.inf)
        ones = jnp.ones((PAGE_SIZE, HEAD_DIM), BF16)

        def k_new(g):
            return kn_ref[0, :, g * HEAD_DIM:(g + 1) * HEAD_DIM].astype(BF16)

        def v_new(g):
            return jnp.concatenate([vn_ref[0, :, g * HEAD_DIM:(g + 1) * HEAD_DIM].astype(BF16), ones], axis=1)

        attend(_select_bias(s_new, lane + n_past, thr_ref[0], cut_ref[0]), k_new, v_new)
        a = acc_ref[...]
        o_ref[0] = (a[:, :HEAD_DIM] / a[:, HEAD_DIM:]).astype(o_ref.dtype)


def _attn_sample(q_hm, qi_hm, kiw, k_f, v_f, cache_k, cache_v, cache_ik, page_table):
    nb, n_pages = page_table.shape
    ts = q_hm.shape[2] // nb
    n_pool = cache_k.shape[0]
    n_past = n_pages * PAGE_SIZE
    topk = min(TOPK_MAX, (n_past + ts) // 4)
    steps = n_pages // PAGES_PER_STEP
    kc = PAGES_PER_STEP * PAGE_SIZE

    def rows_ht(a, nh):
        d = a.shape[-1]
        a = a[0].reshape(nh, nb, ts, d).transpose(1, 0, 2, 3)
        return jnp.pad(a, ((0, 0), (0, 0), (0, TQ - ts), (0, 0))).reshape(nb, nh * TQ, d)

    q_r = rows_ht(q_hm, N_HEADS)
    qi_r = rows_ht(qi_hm, IDX_HEADS)
    wi = kiw[0, :, IDX_DIM:IDX_DIM + IDX_HEADS].reshape(nb, ts, IDX_HEADS).transpose(0, 2, 1)
    wexp = jnp.broadcast_to(jnp.pad(wi, ((0, 0), (0, 0), (0, TQ - ts)))[..., None], (nb, IDX_HEADS, TQ, LANES))
    wexp = wexp.reshape(nb, IDX_HEADS * TQ, LANES)

    def new_page(a):
        return jnp.pad(a[0].reshape(nb, ts, a.shape[-1]), ((0, 0), (0, PAGE_SIZE - ts), (0, 0)))

    ki_new = jnp.swapaxes(new_page(kiw[..., :IDX_DIM]), 1, 2).astype(BF16)
    k_new = new_page(k_f)
    v_new = new_page(v_f)
    cik = jnp.swapaxes(cache_ik, 1, 2)
    ck = cache_k.reshape(n_pool, PAGE_SIZE * KV_HEADS, HEAD_DIM)
    cv = cache_v.reshape(n_pool, PAGE_SIZE * KV_HEADS, HEAD_DIM)

    def page_spec(rows, width, j):
        return pl.BlockSpec((1, rows, width), lambda b, p, pt: (pt[b, p * PAGES_PER_STEP + j], 0, 0))

    def per_b(shape):
        nd = len(shape)
        return pl.BlockSpec((1,) + shape, lambda b, p, pt: (b,) + (0,) * nd)

    sp, sn = pl.pallas_call(
        _sample_scores_kernel, name="sample_scores",
        grid_spec=pltpu.PrefetchScalarGridSpec(
            num_scalar_prefetch=1, grid=(nb, steps),
            in_specs=[per_b((IDX_HEADS * TQ, IDX_DIM)), per_b((IDX_HEADS * TQ, LANES)), per_b((IDX_DIM, PAGE_SIZE))]
            + [page_spec(IDX_DIM, PAGE_SIZE, j) for j in range(PAGES_PER_STEP)],
            out_specs=[pl.BlockSpec((1, TQ, kc), lambda b, p, pt: (b, 0, p)), per_b((TQ, LANES))],
            scratch_shapes=[pltpu.VMEM((IDX_DIM, kc), BF16)]),
        out_shape=[jax.ShapeDtypeStruct((nb, TQ, n_past), F32), jax.ShapeDtypeStruct((nb, TQ, LANES), F32)],
        compiler_params=_cparams(("parallel", "arbitrary")),
    )(page_table, qi_r, wexp, ki_new, *([cik] * PAGES_PER_STEP))

    bt = Q_BLOCK // TQ
    ncol = n_past // LANES + 1
    cw = LANES * max(d for d in range(1, 9) if ncol % d == 0)
    assert nb % bt == 0
    thr, cut = pl.pallas_call(
        functools.partial(_sample_thr_kernel, n_past=n_past, ts=ts, topk=topk, cw=cw),
        grid=(nb // bt,), name="sample_threshold",
        in_specs=[pl.BlockSpec((bt, TQ, n_past), lambda i: (i, 0, 0)), pl.BlockSpec((bt, TQ, LANES), lambda i: (i, 0, 0))],
        out_specs=[pl.BlockSpec((bt, TQ, LANES), lambda i: (i, 0, 0))] * 2,
        out_shape=[jax.ShapeDtypeStruct((nb, TQ, LANES), F32), jax.ShapeDtypeStruct((nb, TQ, LANES), I32)],
        scratch_shapes=[pltpu.VMEM((bt * TQ, n_past + LANES), F32)],
        compiler_params=_cparams(("parallel",)),
    )(sp, sn)

    o = pl.pallas_call(
        functools.partial(_sample_attn_kernel, ts=ts, n_past=n_past), name="sample_attn",
        grid_spec=pltpu.PrefetchScalarGridSpec(
            num_scalar_prefetch=1, grid=(nb, steps),
            in_specs=[per_b((N_HEADS * TQ, HEAD_DIM)), pl.BlockSpec((1, TQ, kc), lambda b, p, pt: (b, 0, p)),
                      per_b((TQ, LANES)), per_b((TQ, LANES)), per_b((TQ, LANES)),
                      per_b((PAGE_SIZE, KV_W)), per_b((PAGE_SIZE, KV_W))]
            + [page_spec(PAGE_SIZE * KV_HEADS, HEAD_DIM, j) for j in range(PAGES_PER_STEP)] * 2,
            out_specs=per_b((N_HEADS * TQ, HEAD_DIM)),
            scratch_shapes=[pltpu.VMEM((KV_HEADS, kc, HEAD_DIM), BF16), pltpu.VMEM((KV_HEADS, kc, 2 * HEAD_DIM), BF16),
                            pltpu.VMEM((N_HEADS * TQ, 1), F32), pltpu.VMEM((N_HEADS * TQ, 2 * HEAD_DIM), F32)]),
        out_shape=jax.ShapeDtypeStruct((nb, N_HEADS * TQ, HEAD_DIM), BF16),
        compiler_params=_cparams(("parallel", "arbitrary")),
    )(page_table, q_r, sp, sn, thr, cut, k_new, v_new, *([ck] * PAGES_PER_STEP), *([cv] * PAGES_PER_STEP))
    o = o.reshape(nb, N_HEADS, TQ, HEAD_DIM)[:, :, :ts].transpose(0, 2, 1, 3)
    return o.reshape(1, nb * ts, ATTN_W)


def _ssd_kernel(xbc_ref, z_ref, dt_ref, dtt_ref, cst_ref, sst_ref, cw_ref, cb_ref, alr_ref, alc_ref, dsk_ref,
                gn_ref, tri_ref, yn_ref, so_ref, xbuf, dtbuf, dttbuf, zbuf, ybuf, state, *, tin, ell):
    c = pl.program_id(1)

    @pl.when(c == 0)
    def _():
        xbuf[0:SUBLANES, :] = cst_ref[0]
        state[...] = sst_ref[0]
        if tin < ell:
            xbuf[SUBLANES:, :] = jnp.zeros((ell, CONV_DIM), F32)
            dtbuf[...] = jnp.zeros(dtbuf.shape, F32)
            dttbuf[...] = jnp.zeros(dttbuf.shape, F32)
            zbuf[...] = jnp.zeros(zbuf.shape, F32)

    if tin < ell:
        xbuf[SUBLANES:SUBLANES + tin, :] = xbc_ref[0]
        dtbuf[0:tin, :] = dt_ref[0]
        dttbuf[:, 0:tin] = dtt_ref[0]
        zbuf[0:tin, :] = z_ref[0]
        dt, dtt, z = dtbuf[...], dttbuf[...], zbuf[...]
    else:
        xbuf[SUBLANES:, :] = xbc_ref[0]
        dt, dtt, z = dt_ref[0], dtt_ref[0], z_ref[0]

    conv = cb_ref[...]
    for j in range(CONV_W):
        conv = conv + xbuf[pl.ds(SUBLANES - (CONV_W - 1) + j, ell), :] * cw_ref[j:j + 1, :]
    act = _silu(conv)
    xbuf[0:SUBLANES, :] = xbuf[ell:ell + SUBLANES, :]

    tri = tri_ref[...]
    a_cum = jnp.dot(tri, dt * (-jnp.exp(alr_ref[...])), precision=lax.Precision.HIGHEST,
                    preferred_element_type=F32)
    a_cum_t = lax.dot_general(dtt * (-jnp.exp(alc_ref[...])), tri, (((1,), (1,)), ((), ())),
                              precision=lax.Precision.HIGHEST, preferred_element_type=F32)
    causal = tri > 0.5
    for g in range(SSM_GROUPS):
        bg = act[:, D_INNER + g * D_STATE:D_INNER + (g + 1) * D_STATE].astype(BF16)
        cg = act[:, D_INNER + (SSM_GROUPS + g) * D_STATE:D_INNER + (SSM_GROUPS + g + 1) * D_STATE].astype(BF16)
        cbm = _dot_nt(cg, bg)
        for r in range(HEADS_PER_GROUP):
            hh = g * HEADS_PER_GROUP + r
            xh = act[:, hh * SSM_HEAD_DIM:(hh + 1) * SSM_HEAD_DIM]
            acol = a_cum[:, hh:hh + 1]
            arow = a_cum_t[hh:hh + 1, :]
            alast = arow[:, ell - 1:ell]
            decay = jnp.exp(jnp.where(causal, acol - arow, -jnp.inf))
            xd = xh * dt[:, hh:hh + 1]
            sh = state[hh]
            y = _dot((cbm * decay).astype(BF16), xd.astype(BF16))
            y = y + _dot_nt(cg, sh.astype(BF16)) * jnp.exp(acol)
            y = y + dsk_ref[:, hh:hh + 1] * xh
            state[hh] = sh * jnp.exp(alast) + _dot_tn((xd * jnp.exp(alast - acol)).astype(BF16), bg)
            ybuf[:, hh * SSM_HEAD_DIM:(hh + 1) * SSM_HEAD_DIM] = y

    y = ybuf[...] * _silu(z)
    gw = D_INNER // SSM_GROUPS
    for g in range(SSM_GROUPS):
        cs = slice(g * gw, (g + 1) * gw)
        yn = (_rms(y[:, cs]) * gn_ref[:, cs]).astype(yn_ref.dtype)
        yn_ref[0, :, cs] = yn[:tin] if tin < ell else yn

    @pl.when(c == pl.num_programs(1) - 1)
    def _():
        so_ref[0] = state[...]


def _ssd(xbc, z, dt, conv_state, ssm_state, conv_w, conv_b, a_log, d_skip, g_norm):
    bsz, t, _ = xbc.shape
    ell = SSD_CHUNK if t >= SSD_CHUNK else SSD_SHORT_CHUNK
    tin = min(t, ell)
    nc = max(t // ell, 1)
    assert t == tin or t % ell == 0
    dtt = jnp.swapaxes(dt, 1, 2)
    cst = jnp.pad(conv_state, ((0, 0), (SUBLANES - (CONV_W - 1), 0), (0, 0)))
    pad_h = ((0, 0), (0, LANES - SSM_HEADS))
    alr = jnp.pad(a_log.reshape(1, SSM_HEADS), pad_h)
    alc = alr.reshape(LANES, 1)
    dsk = jnp.pad(d_skip.reshape(1, SSM_HEADS), pad_h)
    tri = jnp.tril(jnp.ones((ell, ell), F32))
    tok = lambda w: pl.BlockSpec((1, tin, w), lambda b, c: (b, c, 0))
    st_spec = pl.BlockSpec((1, SSM_HEADS, SSM_HEAD_DIM, D_STATE), lambda b, c: (b, 0, 0, 0))
    yn, s_out = pl.pallas_call(
        functools.partial(_ssd_kernel, tin=tin, ell=ell), grid=(bsz, nc), name="ssd",
        in_specs=[tok(CONV_DIM), tok(D_INNER), tok(LANES), pl.BlockSpec((1, LANES, tin), lambda b, c: (b, 0, c)),
                  pl.BlockSpec((1, SUBLANES, CONV_DIM), lambda b, c: (b, 0, 0)), st_spec,
                  _const_spec((CONV_W, CONV_DIM)), _const_spec((1, CONV_DIM)), _const_spec((1, LANES)),
                  _const_spec((LANES, 1)), _const_spec((1, LANES)), _const_spec((1, D_INNER)), _const_spec((ell, ell))],
        out_specs=[tok(D_INNER), st_spec],
        out_shape=[jax.ShapeDtypeStruct((bsz, t, D_INNER), BF16),
                   jax.ShapeDtypeStruct((bsz, SSM_HEADS, SSM_HEAD_DIM, D_STATE), F32)],
        scratch_shapes=[pltpu.VMEM((ell + SUBLANES, CONV_DIM), F32), pltpu.VMEM((ell, LANES), F32),
                        pltpu.VMEM((LANES, ell), F32), pltpu.VMEM((ell, D_INNER), F32),
                        pltpu.VMEM((ell, D_INNER), F32), pltpu.VMEM((SSM_HEADS, SSM_HEAD_DIM, D_STATE), F32)],
        compiler_params=_cparams(("parallel", "arbitrary")),
    )(xbc, z, dt, dtt, cst, ssm_state, conv_w, conv_b.reshape(1, CONV_DIM), alr, alc, dsk,
      g_norm.reshape(1, D_INNER), tri)
    return yn, s_out


def _mix_kernel(x_ref, at_ref, yn_ref, ga_ref, gm_ref, wa_ref, ws_ref, wo_ref, o_ref):
    mix = ga_ref[0] * _dot(at_ref[0], wa_ref[...]) + gm_ref[0] * _dot(yn_ref[0], ws_ref[...])
    o_ref[0] = x_ref[0] + _dot(mix.astype(BF16), wo_ref[...])


def _mix(x, attn, yn, sga, sgm, w_attn_out, w_ssm_out, w_o, tm):
    bsz, t, _ = x.shape
    tok = lambda w: pl.BlockSpec((1, tm, w), lambda b, i: (b, i, 0))
    return pl.pallas_call(
        _mix_kernel, grid=(bsz, t // tm), name="mix",
        in_specs=[tok(D_MODEL), tok(ATTN_W), tok(D_INNER), tok(D_MODEL), tok(D_MODEL),
                  _const_spec((ATTN_W, D_MODEL)), _const_spec((D_INNER, D_MODEL)), _const_spec((D_MODEL, D_MODEL))],
        out_specs=tok(D_MODEL),
        out_shape=jax.ShapeDtypeStruct((bsz, t, D_MODEL), F32),
        compiler_params=_cparams(("parallel", "parallel")),
    )(x, attn, yn, sga, sgm, w_attn_out, w_ssm_out, w_o)


def _route(logits):
    lane = lax.broadcasted_iota(I32, logits.shape, 1)
    ninf = -jnp.inf
    big = jnp.int32(1 << 20)

    def first_max(mask):
        mx = jnp.max(jnp.where(mask, logits, ninf), axis=1, keepdims=True)
        idx = jnp.min(jnp.where(jnp.logical_and(mask, logits == mx), lane, big), axis=1, keepdims=True)
        return mx, idx

    gmask = lane < N_EXPERT_GROUPS
    gmax, gsel = first_max(gmask)
    g_prob = 1.0 / jnp.sum(jnp.where(gmask, jnp.exp(logits - gmax), 0.0), axis=1, keepdims=True)
    lo = N_EXPERT_GROUPS + gsel * EXPERTS_PER_GROUP
    emask = jnp.logical_and(lane >= lo, lane < lo + EXPERTS_PER_GROUP)
    m1, i1 = first_max(emask)
    m2, i2 = first_max(jnp.logical_and(emask, lane != i1))
    zsum = jnp.sum(jnp.where(emask, jnp.exp(logits - m1), 0.0), axis=1, keepdims=True)
    p1 = 1.0 / zsum
    p2 = jnp.exp(m2 - m1) / zsum
    w1 = p1 / (p1 + p2) * g_prob
    w2 = p2 / (p1 + p2) * g_prob
    return jnp.where(lane == i1, w1, 0.0) + jnp.where(lane == i2, w2, 0.0)


def _moe_kernel(x_ref, pe_ref, gf_ref, wr_ref, br_ref, wg_ref, wu_ref, wd_ref, wp_ref, gp_ref, wpg_ref, gfin_ref,
                y_ref, hb_ref, gate_ref, acc_ref, *, final_norm):
    e = pl.program_id(2)

    @pl.when(e == 0)
    def _():
        h = _rms(x_ref[0]) * gf_ref[...]
        hb_ref[...] = h.astype(BF16)
        logits = jnp.dot(h, wr_ref[...], precision=lax.Precision.HIGHEST, preferred_element_type=F32) + br_ref[...]
        gate_ref[...] = _route(logits)
        acc_ref[...] = jnp.zeros(acc_ref.shape, F32)

    hb = hb_ref[...]
    act = _silu(_dot(hb, wg_ref[0])) * _dot(hb, wu_ref[0])
    lane = lax.broadcasted_iota(I32, gate_ref.shape, 1)
    ge = jnp.sum(jnp.where(lane == e + N_EXPERT_GROUPS, gate_ref[...], 0.0), axis=1, keepdims=True)
    acc_ref[...] += ge * _dot(act.astype(BF16), wd_ref[0])

    @pl.when(e == pl.num_programs(2) - 1)
    def _():
        x2 = x_ref[0] + acc_ref[...]
        gate = jax.nn.sigmoid(_dot((_rms(x2) * gp_ref[...]).astype(BF16), wpg_ref[...]))
        x3 = x2 + _dot(pe_ref[0].astype(BF16), wp_ref[...]) * gate
        y_ref[0] = _rms(x3) * gfin_ref[...] if final_norm else x3


def _moe(x, p_emb, g_ffn, w_router, b_router, w_gate, w_up, w_down, w_ple, g_ple, w_ple_gate, g_final, tm,
         final_norm):
    bsz, t, _ = x.shape
    tok = lambda w: pl.BlockSpec((1, tm, w), lambda b, i, e: (b, i, 0))
    cst = _const_spec
    return pl.pallas_call(
        functools.partial(_moe_kernel, final_norm=final_norm), grid=(bsz, t // tm, N_EXPERTS), name="moe",
        in_specs=[tok(D_MODEL), tok(PLE_DIM), cst((1, D_MODEL)), cst((D_MODEL, LANES)), cst((1, LANES)),
                  pl.BlockSpec((1, D_MODEL, EXPERT_FF), lambda b, i, e: (e, 0, 0)),
                  pl.BlockSpec((1, D_MODEL, EXPERT_FF), lambda b, i, e: (e, 0, 0)),
                  pl.BlockSpec((1, EXPERT_FF, D_MODEL), lambda b, i, e: (e, 0, 0)),
                  cst((PLE_DIM, D_MODEL)), cst((1, D_MODEL)), cst((D_MODEL, D_MODEL)), cst((1, D_MODEL))],
        out_specs=tok(D_MODEL),
        out_shape=jax.ShapeDtypeStruct((bsz, t, D_MODEL), F32),
        scratch_shapes=[pltpu.VMEM((tm, D_MODEL), BF16), pltpu.VMEM((tm, LANES), F32), pltpu.VMEM((tm, D_MODEL), F32)],
        compiler_params=_cparams(("parallel", "parallel", "arbitrary")),
    )(x, p_emb, g_ffn, w_router, b_router, w_gate, w_up, w_down, w_ple, g_ple, w_ple_gate, g_final)


def _rope_tables(pos):
    posf = pos.astype(F32)[:, None]
    n = pos.shape[0]

    def cs(rot):
        half = rot // 2
        inv = ROPE_THETA ** (-(jnp.arange(half, dtype=F32) * 2.0 / rot))
        ang = posf * inv[None, :]
        return jnp.cos(ang), jnp.sin(ang), half

    def pattern(width, rot):
        cos, sin, half = cs(rot)
        ones = jnp.ones((n, width - rot), F32)
        zeros = jnp.zeros((n, width - rot), F32)
        zh = jnp.zeros((n, half), F32)
        return (jnp.concatenate([cos, cos, ones], 1), jnp.concatenate([-sin, zh, zeros], 1),
                jnp.concatenate([zh, sin, zeros], 1))

    c128, s1_128, s2_128 = pattern(HEAD_DIM, HEAD_DIM // 4)
    c64, s1_64, s2_64 = pattern(IDX_DIM, IDX_DIM // 4)
    two = lambda a: jnp.concatenate([a, a], 1)
    wscale = jnp.full((n, IDX_HEADS), (IDX_HEADS * IDX_DIM) ** -0.5, F32)
    rest1 = jnp.ones((n, LANES - IDX_DIM - IDX_HEADS), F32)
    z64 = jnp.zeros((n, LANES - IDX_DIM), F32)
    return jnp.stack([c128, s1_128, s2_128, two(c64), two(s1_64), two(s2_64),
                      jnp.concatenate([c64, wscale, rest1], 1), jnp.concatenate([s1_64, z64], 1),
                      jnp.concatenate([s2_64, z64], 1)])


def _relayout_w_in(w_in, dt_bias):
    offs = [int(o) for o in np.cumsum(IN_SPLITS)[:-1]]
    wq, wk, wv, wqi, wki, wwi, wz, wxbc, wdt, wga, wgm = jnp.split(w_in, offs, axis=1)
    padc = lambda a, n: jnp.pad(a, ((0, 0), (0, n - a.shape[1])))
    w_a = jnp.concatenate([wq, wk, wv, wqi, padc(jnp.concatenate([wki, wwi], 1), LANES)], 1).astype(BF16)
    w_b = jnp.concatenate([wz, wxbc, padc(wdt, LANES)], 1).astype(BF16)
    w_c = jnp.concatenate([wga, wgm], 1).astype(BF16)
    dtb = jnp.pad(dt_bias.reshape(1, SSM_HEADS), ((0, 0), (0, LANES - SSM_HEADS)))
    return w_a, w_b, w_c, dtb


def kernel(x_prompt, x_sample, cache_k, cache_v, cache_idx_k, state_ssm, state_conv, page_table, p_prompt, p_sample, g_mix, w_in, w_attn_out, conv_w, conv_b, dt_bias, a_log, d_skip, g_ssm_norm, w_ssm_out, w_o, g_ffn, w_group_router, b_group_router, w_expert_router, b_expert_router, w_expert_gate, w_expert_up, w_expert_down, w_ple, g_ple, w_ple_gate, g_final):
    depth = w_in.shape[0]
    bp, tp, _ = x_prompt.shape
    nb, ts, _ = x_sample.shape
    past = page_table.shape[1] * PAGE_SIZE
    row = lambda a: a.reshape(1, -1)
    tabs_p = _rope_tables(jnp.arange(tp))
    tabs_s = _rope_tables(past + (jnp.arange(nb * ts) % ts))
    conv0 = jnp.zeros((bp, CONV_W - 1, CONV_DIM), F32)
    ssm0 = jnp.zeros((bp, SSM_HEADS, SSM_HEAD_DIM, D_STATE), F32)
    xp = x_prompt
    xs = x_sample.reshape(1, nb * ts, D_MODEL)
    outs = [[] for _ in range(10)]
    for i in range(depth):
        w_a, w_b, w_c, dtb = _relayout_w_in(w_in[i], dt_bias[i])
        wao, wso, wo = w_attn_out[i].astype(BF16), w_ssm_out[i].astype(BF16), w_o[i].astype(BF16)
        w_router = jnp.pad(jnp.concatenate([w_group_router[i], w_expert_router[i]], 1),
                           ((0, 0), (0, LANES - N_EXPERT_GROUPS - N_EXPERTS)))
        b_router = jnp.pad(jnp.concatenate([b_group_router[i], b_expert_router[i]]).reshape(1, -1),
                           ((0, 0), (0, LANES - N_EXPERT_GROUPS - N_EXPERTS)))
        wg, wu, wd = w_expert_gate[i].astype(BF16), w_expert_up[i].astype(BF16), w_expert_down[i].astype(BF16)
        wple, wpleg = w_ple[i].astype(BF16), w_ple_gate[i].astype(BF16)

        last_layer = i == depth - 1

        def layer(x, p_emb, tabs, attn_fn, conv_state, ssm_state, batch_shape, tm, tm_moe):
            (q_hm, k_f, v_f, k_bf, v_bf, qi_hm, kiw, ki_bf), (z, xbc, dt), (sga, sgm) = _in_proj(
                x, row(g_mix[i]), w_a, w_b, w_c, tabs, dtb, tm)
            attn = attn_fn(q_hm, qi_hm, kiw, ki_bf, k_f, v_f, k_bf, v_bf)
            sb, st = batch_shape
            seq = lambda a: a.reshape(sb, st, a.shape[-1])
            yn, s_new = _ssd(seq(xbc), seq(z), seq(dt), conv_state, ssm_state, conv_w[i], conv_b[i], a_log[i],
                             d_skip[i], g_ssm_norm[i])
            x1 = _mix(x, attn, yn.reshape(x.shape[0], x.shape[1], D_INNER), sga, sgm, wao, wso, wo, tm)
            x2 = _moe(x1, p_emb, row(g_ffn[i]), w_router, b_router, wg, wu, wd, wple, row(g_ple[i]), wpleg,
                      row(g_final), tm_moe, last_layer)
            assert st >= CONV_W - 1
            conv_new = seq(xbc)[:, st - (CONV_W - 1):]
            return x2, k_f, v_f, kiw[..., :IDX_DIM], conv_new, s_new

        attn_p = lambda q_hm, qi_hm, kiw, ki_bf, k_f, v_f, k_bf, v_bf: _attn_prompt(q_hm, qi_hm, kiw, ki_bf, k_bf, v_bf)
        attn_s = lambda q_hm, qi_hm, kiw, ki_bf, k_f, v_f, k_bf, v_bf: _attn_sample(
            q_hm, qi_hm, kiw, k_f, v_f, cache_k[i], cache_v[i], cache_idx_k[i], page_table)
        xp, kp, vp, ikp, convp, ssmp = layer(xp, p_prompt[i], tabs_p, attn_p, conv0, ssm0, (bp, tp), 512, 1024)
        xs, ks, vs, iks, convs, ssms = layer(xs, p_sample[i].reshape(1, nb * ts, PLE_DIM), tabs_s, attn_s,
                                             state_conv[i], state_ssm[i], (nb, ts), nb * ts, nb * ts)
        vals = [kp.reshape(bp, tp, KV_HEADS, HEAD_DIM), vp.reshape(bp, tp, KV_HEADS, HEAD_DIM), ikp,
                ks.reshape(nb, ts, KV_HEADS, HEAD_DIM), vs.reshape(nb, ts, KV_HEADS, HEAD_DIM),
                iks.reshape(nb, ts, IDX_DIM), ssmp, convp, ssms, convs]
        for lst, v in zip(outs, vals):
            lst.append(v)
    return (xp, xs.reshape(nb, ts, D_MODEL)) + tuple(jnp.stack(lst) for lst in outs)
```
